```python
import math
import jax, jax.numpy as jnp
from jax import lax
import numpy as np

D_MODEL = 1024
BATCH = 32
SEQ = 256
DEPTH = 4
DEC_BATCH = 8
DEC_SEQ = 4096
PAST_LEN = 512

GRID_W = 64
N_MIXERS = 4
QB = 128
ROPE_BASE = 10000.0
LN_EPS = 1e-5
NEG_INF = -1e30
ALPHA = (2 * DEPTH) ** 0.25
BETA = (8 * DEPTH) ** -0.25
D_FF = 4 * D_MODEL
DIFF_DH = 64
DIFF_HEADS = D_MODEL // (2 * DIFF_DH)
DIFF_LAMBDA_INIT = 0.8 - 0.6 * math.exp(-0.3 * 0)
NA_DH = 64
NA_HEADS = D_MODEL // NA_DH
NA_KH = 8
NA_KW = 16
MLA_HEADS = 16
MLA_NOPE = 64
MLA_ROPE = 32
MLA_VDIM = 64
MLA_Q_RANK = D_MODEL // 2
MLA_KV_RANK = D_MODEL // 4
SWA_DH = 64
SWA_HEADS = D_MODEL // SWA_DH
SWA_KV_HEADS = 4
SWA_WINDOW = 128

kernel_name = 'hybrid_diffusion_interleaved_step'


def layer_norm(x, g, b):
    xf = x.astype(jnp.float32)
    mu = jnp.mean(xf, -1, keepdims=True)
    var = jnp.mean(jnp.square(xf - mu), -1, keepdims=True)
    return ((xf - mu) * lax.rsqrt(var + LN_EPS) * g + b).astype(x.dtype)


def rms_norm(x, g):
    xf = x.astype(jnp.float32)
    return (xf * lax.rsqrt(jnp.mean(xf * xf, -1, keepdims=True) + LN_EPS) * g).astype(x.dtype)


def axial_rope(x):
    T, R = x.shape[1], x.shape[-1]
    n = R // 4
    t = jnp.arange(T)
    inv = ROPE_BASE ** (-jnp.arange(n, dtype=jnp.float32) / n)
    bshape = (T,) + (1,) * (x.ndim - 3) + (n,)

    def rot(xh, pos):
        ang = (pos.astype(jnp.float32)[:, None] * inv).reshape(bshape)
        cos, sin = jnp.cos(ang).astype(x.dtype), jnp.sin(ang).astype(x.dtype)
        x1, x2 = xh[..., :n], xh[..., n:]
        return jnp.concatenate([x1 * cos - x2 * sin, x2 * cos + x1 * sin], -1)

    half = R // 2
    return jnp.concatenate([rot(x[..., :half], t // GRID_W), rot(x[..., half:], t % GRID_W)], -1)


def map_query_blocks(fn, *qs):
    B, T = qs[0].shape[:2]
    nb = T // QB
    blocks = tuple(jnp.moveaxis(a.reshape((B, nb, QB) + a.shape[2:]), 1, 0) for a in qs)
    out = lax.map(lambda args: fn(*args), blocks)
    return jnp.moveaxis(out, 0, 1).reshape((B, T) + out.shape[3:])


def dense_attend(q, k, v, sink=None):
    B, T, Hq, dh = q.shape
    Hk = k.shape[2]
    G = Hq // Hk
    scale = dh ** -0.5

    def block(qb):
        qg = qb.reshape(B, QB, Hk, G, dh)
        logits = jnp.einsum('bqkgd,bskd->bkgqs', qg, k, preferred_element_type=jnp.float32) * scale
        if sink is not None:
            s = jnp.broadcast_to(sink.astype(jnp.float32).reshape(1, Hk, G, 1, 1), logits.shape[:-1] + (1,))
            p = jax.nn.softmax(jnp.concatenate([logits, s], -1), -1)[..., :-1]
        else:
            p = jax.nn.softmax(logits, -1)
        o = jnp.einsum('bkgqs,bskd->bqkgd', p, v)
        return o.reshape(B, QB, Hq, dh).astype(v.dtype)

    return map_query_blocks(block, q)


def modulation(cond, w, b):
    m = jax.nn.silu(cond) @ w + b
    return m.reshape(m.shape[:-1] + (6, D_MODEL))


def modulate(x, shift, scale):
    return x * (1 + scale) + shift


def sq_relu_mlp(h, w1, w2):
    return jnp.square(jax.nn.relu(h @ w1)) @ w2


def diff_qkv(h, w_qkv):
    B, T, _ = h.shape
    q, k, v = jnp.split(h @ w_qkv, 3, -1)
    q = q.reshape(B, T, DIFF_HEADS, 2, DIFF_DH)
    k = k.reshape(B, T, DIFF_HEADS, 2, DIFF_DH)
    v = v.reshape(B, T, DIFF_HEADS, 2 * DIFF_DH)
    return q, k, v


def diff_lambda(lam):
    lf = lam.astype(jnp.float32)
    return jnp.exp(jnp.sum(lf[0] * lf[1])) - jnp.exp(jnp.sum(lf[2] * lf[3])) + DIFF_LAMBDA_INIT


def diff_attend(q, k, v, lam, subln):
    B, T = q.shape[:2]
    scale = DIFF_DH ** -0.5

    def block(qb):
        logits = jnp.einsum('bqhmd,bkhmd->bhmqk', qb, k, preferred_element_type=jnp.float32) * scale
        p = jax.nn.softmax(logits, -1)
        a = p[:, :, 0] - lam * p[:, :, 1]
        return jnp.einsum('bhqk,bkhe->bqhe', a, v).astype(v.dtype)

    o = map_query_blocks(block, q)
    o = rms_norm(o, subln) * (1.0 - DIFF_LAMBDA_INIT)
    return o.reshape(B, T, D_MODEL)


def diff_ctx(h, w_qkv, lam, subln, w_o):
    B, S, _ = h.shape
    q, k, v = diff_qkv(h, w_qkv)
    o = diff_attend(q, k, v, diff_lambda(lam), subln)
    return o @ w_o, (k.reshape(B, S, DIFF_HEADS, 2 * DIFF_DH), v)


def diff_lat(h, ck, cv, w_qkv, lam, subln, w_o):
    B, P = ck.shape[:2]
    q, k, v = diff_qkv(h, w_qkv)
    q, k = axial_rope(q), axial_rope(k)
    k_all = jnp.concatenate([ck.reshape(B, P, DIFF_HEADS, 2, DIFF_DH), k], 1)
    v_all = jnp.concatenate([cv, v], 1)
    return diff_attend(q, k_all, v_all, diff_lambda(lam), subln) @ w_o


def na_qkv(h, w_qkv):
    B, T, _ = h.shape
    q, k, v = jnp.split(h @ w_qkv, 3, -1)
    shp = (B, T, NA_HEADS, NA_DH)
    return q.reshape(shp), k.reshape(shp), v.reshape(shp)


def na_ctx(h, w_qkv, rpb, w_o):
    B, S, _ = h.shape
    q, k, v = na_qkv(h, w_qkv)
    o = dense_attend(q, k, v)
    return o.reshape(B, S, D_MODEL) @ w_o, (k, v)


def na_lat(h, ck, cv, w_qkv, rpb, w_o):
    B, T, _ = h.shape
    rows = T // GRID_W
    kh = min(NA_KH, rows)
    q, k, v = na_qkv(h, w_qkv)
    grid = lambda a: a.reshape(B, rows, GRID_W, NA_HEADS, NA_DH)
    qg, kg, vg = grid(q), grid(k), grid(v)
    cols = np.arange(GRID_W)
    c0 = np.clip(cols - NA_KW // 2, 0, GRID_W - NA_KW)
    col_idx = c0[:, None] + np.arange(NA_KW)
    col_rel = col_idx - cols[:, None] + NA_KW - 1
    n_nb = kh * NA_KW
    scale = NA_DH ** -0.5

    def row_block(args):
        r, qr = args
        r0 = jnp.clip(r - kh // 2, 0, rows - kh)
        ks = lax.dynamic_slice_in_dim(kg, r0, kh, axis=1)[:, :, col_idx]
        vs = lax.dynamic_slice_in_dim(vg, r0, kh, axis=1)[:, :, col_idx]
        row_rel = r0 - r + jnp.arange(kh) + NA_KH - 1
        bias = rpb[:, row_rel][:, :, col_rel]
        ln = jnp.einsum('bchd,bicjhd->bhcij', qr, ks, preferred_element_type=jnp.float32) * scale
        ln = ln + jnp.transpose(bias, (0, 2, 1, 3)).astype(jnp.float32)[None]
        lc = jnp.einsum('bchd,bshd->bhcs', qr, ck, preferred_element_type=jnp.float32) * scale
        p = jax.nn.softmax(jnp.concatenate([ln.reshape(B, NA_HEADS, GRID_W, n_nb), lc], -1), -1)
        pn = p[..., :n_nb].reshape(B, NA_HEADS, GRID_W, kh, NA_KW)
        o = jnp.einsum('bhcij,bicjhd->bchd', pn, vs) + jnp.einsum('bhcs,bshd->bchd', p[..., n_nb:], cv)
        return o.astype(h.dtype)

    o = lax.map(row_block, (jnp.arange(rows), jnp.moveaxis(qg, 1, 0)))
    return jnp.moveaxis(o, 0, 1).reshape(B, T, D_MODEL) @ w_o


def mla_project(h, w_a, q_norm, kv_norm, w_uq):
    B, T, _ = h.shape
    a = h @ w_a
    cq = a[..., :MLA_Q_RANK]
    ckv = a[..., MLA_Q_RANK:MLA_Q_RANK + MLA_KV_RANK]
    kpe = a[..., MLA_Q_RANK + MLA_KV_RANK:]
    q = (rms_norm(cq, q_norm) @ w_uq).reshape(B, T, MLA_HEADS, MLA_NOPE + MLA_ROPE)
    return q[..., :MLA_NOPE], q[..., MLA_NOPE:], rms_norm(ckv, kv_norm), kpe


def mla_expand(ckv, w_ukv):
    B, T, _ = ckv.shape
    kv = (ckv @ w_ukv).reshape(B, T, MLA_HEADS, MLA_NOPE + MLA_VDIM)
    return kv[..., :MLA_NOPE], kv[..., MLA_NOPE:]


def mla_attend(qn, qp, kn, kp, v):
    scale = (MLA_NOPE + MLA_ROPE) ** -0.5

    def block(qnb, qpb):
        logits = (jnp.einsum('bqhd,bkhd->bhqk', qnb, kn, preferred_element_type=jnp.float32)
                  + jnp.einsum('bqhr,bkr->bhqk', qpb, kp, preferred_element_type=jnp.float32)) * scale
        p = jax.nn.softmax(logits, -1)
        return jnp.einsum('bhqk,bkhd->bqhd', p, v).astype(v.dtype)

    return map_query_blocks(block, qn, qp)


def mla_ctx(h, w_a, q_norm, kv_norm, w_uq, w_ukv, w_o):
    B, S, _ = h.shape
    qn, qp, ckv, kpe = mla_project(h, w_a, q_norm, kv_norm, w_uq)
    kn, v = mla_expand(ckv, w_ukv)
    o = mla_attend(qn, qp, kn, kpe, v)
    return o.reshape(B, S, MLA_HEADS * MLA_VDIM) @ w_o, (ckv, kpe)


def mla_lat(h, c_ckv, c_kpe, w_a, q_norm, kv_norm, w_uq, w_ukv, w_o):
    B, T, _ = h.shape
    qn, qp, ckv, kpe = mla_project(h, w_a, q_norm, kv_norm, w_uq)
    qp = axial_rope(qp)
    kpe = axial_rope(kpe[:, :, None, :])[:, :, 0]
    kn, v = mla_expand(jnp.concatenate([c_ckv, ckv], 1), w_ukv)
    o = mla_attend(qn, qp, kn, jnp.concatenate([c_kpe, kpe], 1), v)
    return o.reshape(B, T, MLA_HEADS * MLA_VDIM) @ w_o


def swa_qkv(h, w_qkv):
    B, T, _ = h.shape
    a = h @ w_qkv
    nq, nk = SWA_HEADS * SWA_DH, SWA_KV_HEADS * SWA_DH
    q = a[..., :nq].reshape(B, T, SWA_HEADS, SWA_DH)
    k = a[..., nq:nq + nk].reshape(B, T, SWA_KV_HEADS, SWA_DH)
    v = a[..., nq + nk:].reshape(B, T, SWA_KV_HEADS, SWA_DH)
    return q, k, v


def swa_ctx(h, w_qkv, sink, w_o):
    B, S, _ = h.shape
    q, k, v = swa_qkv(h, w_qkv)
    o = dense_attend(q, k, v, sink)
    return o.reshape(B, S, SWA_HEADS * SWA_DH) @ w_o, (k, v)


def swa_lat(h, ck, cv, w_qkv, sink, w_o):
    B, T, _ = h.shape
    P = ck.shape[1]
    nb = T // QB
    G = SWA_HEADS // SWA_KV_HEADS
    scale = SWA_DH ** -0.5
    q, k, v = swa_qkv(h, w_qkv)
    q, k = axial_rope(q), axial_rope(k)

    def band(a):
        ab = a.reshape((B, nb, QB) + a.shape[2:])
        ap = jnp.pad(ab, [(0, 0), (1, 1)] + [(0, 0)] * (ab.ndim - 2))
        return jnp.moveaxis(jnp.concatenate([ap[:, :-2], ap[:, 1:-1], ap[:, 2:]], axis=2), 1, 0)

    qi = np.arange(QB)[:, None]
    kj = np.arange(3 * QB)[None, :]
    kpos = (np.arange(nb)[:, None, None] - 1) * QB + kj[None]
    mask = (np.abs(kj - QB - qi) <= SWA_WINDOW)[None] & (kpos >= 0) & (kpos < T)
    sink_l = sink.astype(jnp.float32).reshape(1, SWA_KV_HEADS, G, 1, 1)

    def block(args):
        qb, kb, vb, mb = args
        qg = qb.reshape(B, QB, SWA_KV_HEADS, G, SWA_DH)
        lw = jnp.einsum('bqkgd,bskd->bkgqs', qg, kb, preferred_element_type=jnp.float32) * scale
        lw = jnp.where(mb, lw, NEG_INF)
        lc = jnp.einsum('bqkgd,bskd->bkgqs', qg, ck, preferred_element_type=jnp.float32) * scale
        ls = jnp.broadcast_to(sink_l, lw.shape[:-1] + (1,))
        p = jax.nn.softmax(jnp.concatenate([lw, lc, ls], -1), -1)
        o = (jnp.einsum('bkgqs,bskd->bqkgd', p[..., :3 * QB], vb)
             + jnp.einsum('bkgqs,bskd->bqkgd', p[..., 3 * QB:3 * QB + P], cv))
        return o.reshape(B, QB, SWA_HEADS, SWA_DH).astype(h.dtype)

    qb = jnp.moveaxis(q.reshape(B, nb, QB, SWA_HEADS, SWA_DH), 1, 0)
    o = lax.map(block, (qb, band(k), band(v), jnp.asarray(mask)))
    return jnp.moveaxis(o, 0, 1).reshape(B, T, SWA_HEADS * SWA_DH) @ w_o


def setup_inputs(seed: int = 0) -> dict:
    key = jax.random.key(seed)
    ks = iter(jax.random.split(key, 48))

    def nrm(shape, scale=1.0):
        return jax.random.normal(next(ks), shape, jnp.float32) * scale

    D = D_MODEL
    inv = D ** -0.5
    return {
        'x_prompt': nrm((BATCH, SEQ, D)),
        'x_sample': nrm((DEC_BATCH, DEC_SEQ, D)),
        'cache_l0_k': nrm((DEC_BATCH, PAST_LEN, DIFF_HEADS, 2 * DIFF_DH)),
        'cache_l0_v': nrm((DEC_BATCH, PAST_LEN, DIFF_HEADS, 2 * DIFF_DH)),
        'cache_l1_k': nrm((DEC_BATCH, PAST_LEN, NA_HEADS, NA_DH)),
        'cache_l1_v': nrm((DEC_BATCH, PAST_LEN, NA_HEADS, NA_DH)),
        'cache_l2_ckv': nrm((DEC_BATCH, PAST_LEN, MLA_KV_RANK)),
        'cache_l2_kpe': nrm((DEC_BATCH, PAST_LEN, MLA_ROPE)),
        'cache_l3_k': nrm((DEC_BATCH, PAST_LEN, SWA_KV_HEADS, SWA_DH)),
        'cache_l3_v': nrm((DEC_BATCH, PAST_LEN, SWA_KV_HEADS, SWA_DH)),
        'c': nrm((DEC_BATCH, D)),
        'c_ctx': nrm((D,)),
        'w_mod': nrm((DEPTH, D, 6 * D), inv),
        'b_mod': nrm((DEPTH, 6 * D), 0.02),
        'ln_g': 1.0 + nrm((DEPTH, 2, D), 0.02),
        'ln_b': nrm((DEPTH, 2, D), 0.02),
        'w_mlp1': nrm((DEPTH, D, D_FF), inv),
        'w_mlp2': nrm((DEPTH, D_FF, D), BETA * D_FF ** -0.5),
        'l0_w_qkv': nrm((D, 3 * D), inv),
        'l0_lam': nrm((4, DIFF_DH), 0.1),
        'l0_subln': 1.0 + nrm((2 * DIFF_DH,), 0.02),
        'l0_w_o': nrm((D, D), BETA * inv),
        'l1_w_qkv': nrm((D, 3 * D), inv),
        'l1_rpb': nrm((NA_HEADS, 2 * NA_KH - 1, 2 * NA_KW - 1), 0.1),
        'l1_w_o': nrm((D, D), BETA * inv),
        'l2_w_a': nrm((D, MLA_Q_RANK + MLA_KV_RANK + MLA_ROPE), inv),
        'l2_q_norm': 1.0 + nrm((MLA_Q_RANK,), 0.02),
        'l2_kv_norm': 1.0 + nrm((MLA_KV_RANK,), 0.02),
        'l2_w_uq': nrm((MLA_Q_RANK, MLA_HEADS * (MLA_NOPE + MLA_ROPE)), MLA_Q_RANK ** -0.5),
        'l2_w_ukv': nrm((MLA_KV_RANK, MLA_HEADS * (MLA_NOPE + MLA_VDIM)), MLA_KV_RANK ** -0.5),
        'l2_w_o': nrm((MLA_HEADS * MLA_VDIM, D), BETA * (MLA_HEADS * MLA_VDIM) ** -0.5),
        'l3_w_qkv': nrm((D, (SWA_HEADS + 2 * SWA_KV_HEADS) * SWA_DH), inv),
        'l3_sink': nrm((SWA_HEADS,), 0.5),
        'l3_w_o': nrm((SWA_HEADS * SWA_DH, D), BETA * (SWA_HEADS * SWA_DH) ** -0.5),
    }


def reference(x_prompt, x_sample, cache_l0_k, cache_l0_v, cache_l1_k, cache_l1_v, cache_l2_ckv,
              cache_l2_kpe, cache_l3_k, cache_l3_v, c, c_ctx, w_mod, b_mod, ln_g, ln_b, w_mlp1, w_mlp2,
              l0_w_qkv, l0_lam, l0_subln, l0_w_o, l1_w_qkv, l1_rpb, l1_w_o, l2_w_a, l2_q_norm,
              l2_kv_norm, l2_w_uq, l2_w_ukv, l2_w_o, l3_w_qkv, l3_sink, l3_w_o):
    ctx_mixers = (diff_ctx, na_ctx, mla_ctx, swa_ctx)
    lat_mixers = (diff_lat, na_lat, mla_lat, swa_lat)
    params = ((l0_w_qkv, l0_lam, l0_subln, l0_w_o),
              (l1_w_qkv, l1_rpb, l1_w_o),
              (l2_w_a, l2_q_norm, l2_kv_norm, l2_w_uq, l2_w_ukv, l2_w_o),
              (l3_w_qkv, l3_sink, l3_w_o))
    caches = ((cache_l0_k, cache_l0_v), (cache_l1_k, cache_l1_v),
              (cache_l2_ckv, cache_l2_kpe), (cache_l3_k, cache_l3_v))
    new_state = []
    xp, xs = x_prompt, x_sample
    for i in range(DEPTH):
        m = i % N_MIXERS
        mp = modulation(c_ctx, w_mod[i], b_mod[i])
        ms = modulation(c, w_mod[i], b_mod[i])[:, :, None, :]
        y, state = ctx_mixers[m](modulate(xp, mp[0], mp[1]), *params[i])
        xp = layer_norm(ALPHA * xp + mp[2] * y, ln_g[i, 0], ln_b[i, 0])
        y = sq_relu_mlp(modulate(xp, mp[3], mp[4]), w_mlp1[i], w_mlp2[i])
        xp = layer_norm(ALPHA * xp + mp[5] * y, ln_g[i, 1], ln_b[i, 1])
        new_state.extend(state)
        y = lat_mixers[m](modulate(xs, ms[:, 0], ms[:, 1]), *caches[i], *params[i])
        xs = layer_norm(ALPHA * xs + ms[:, 2] * y, ln_g[i, 0], ln_b[i, 0])
        y = sq_relu_mlp(modulate(xs, ms[:, 3], ms[:, 4]), w_mlp1[i], w_mlp2[i])
        xs = layer_norm(ALPHA * xs + ms[:, 5] * y, ln_g[i, 1], ln_b[i, 1])
    return (xp, xs, *new_state)
```

```python
import functools
import math

import numpy as np
import jax
import jax.numpy as jnp
from jax import lax
from jax.experimental import pallas as pl
from jax.experimental.pallas import tpu as pltpu

D_MODEL = 1024
DEPTH = 4
GRID_W = 64
ROPE_BASE = 10000.0
LN_EPS = 1e-5
NEG_INF = -1e30
ALPHA = (2 * DEPTH) ** 0.25
D_FF = 4 * D_MODEL
DIFF_DH = 64
DIFF_HEADS = 8
DIFF_LAMBDA_INIT = 0.8 - 0.6 * math.exp(-0.3 * 0)
NA_HEADS = 16
NA_KH = 8
NA_KW = 16
MLA_HEADS = 16
MLA_NOPE = 64
MLA_ROPE = 32
MLA_VDIM = 64
MLA_Q_RANK = 512
MLA_KV_RANK = 256
SWA_HEADS = 16
SWA_KV_HEADS = 4
SWA_WINDOW = 128
SWA_QB = 128

LANES = 128
HALF = LANES // 2
MOD_ROWS = 16
VMEM_LIMIT = 56 * 1024 * 1024

F32 = jnp.float32
BF16 = jnp.bfloat16


def _cparams(sem):
    return pltpu.CompilerParams(dimension_semantics=sem, vmem_limit_bytes=VMEM_LIMIT)


def _dot(a, b):
    return jnp.dot(a, b, preferred_element_type=F32)


def _dot_nt(a, b):
    return lax.dot_general(a, b, (((1,), (1,)), ((), ())), preferred_element_type=F32)


def _layer_norm(x, g, b):
    mu = jnp.mean(x, -1, keepdims=True)
    xc = x - mu
    var = jnp.mean(xc * xc, -1, keepdims=True)
    return xc * lax.rsqrt(var + LN_EPS) * g + b


def _rms_norm(x, g):
    return x * lax.rsqrt(jnp.mean(x * x, -1, keepdims=True) + LN_EPS) * g


def _rope_slab(x, cos, sin_up, sin_dn, n):
    return x * cos + pltpu.roll(x, LANES - n, 1) * sin_up + pltpu.roll(x, n, 1) * sin_dn


def _lane_lo(shape):
    return lax.broadcasted_iota(jnp.int32, shape, len(shape) - 1) < HALF


def _split_heads(q):
    lo = _lane_lo(q.shape)
    zero = jnp.zeros_like(q)
    return jnp.concatenate([jnp.where(lo, q, zero), jnp.where(lo, zero, q)], axis=0)


def _mod_kernel(c_ref, w_ref, b_ref, o_ref):
    c = c_ref[...]
    s = c / (1.0 + jnp.exp(-c))
    o_ref[0] = _dot(s.astype(BF16), w_ref[0].astype(BF16)) + b_ref[0]


def _modulation(cond, w_mod, b_mod):
    tn = 512
    n = 6 * D_MODEL
    out = pl.pallas_call(
        _mod_kernel,
        grid=(DEPTH, n // tn),
        in_specs=[pl.BlockSpec((MOD_ROWS, D_MODEL), lambda l, j: (0, 0)),
                  pl.BlockSpec((1, D_MODEL, tn), lambda l, j: (l, 0, j)),
                  pl.BlockSpec((1, 1, tn), lambda l, j: (l, 0, j))],
        out_specs=pl.BlockSpec((1, MOD_ROWS, tn), lambda l, j: (l, 0, j)),
        out_shape=jax.ShapeDtypeStruct((DEPTH, MOD_ROWS, n), F32),
        compiler_params=_cparams(("parallel", "parallel")),
        name="modulation",
    )(cond, w_mod, b_mod.reshape(DEPTH, 1, n))
    return out.reshape(DEPTH, MOD_ROWS, 6, D_MODEL)


def _proj_kernel(*refs, segs, rope_n, n_state, cw):
    x_ref, mod_ref, w_ref = refs[:3]
    pos = 3
    if rope_n:
        cos_ref, sup_ref, sdn_ref = refs[3:6]
        pos = 6
    outs = refs[pos:pos + len(segs)]
    states = refs[pos + len(segs):]
    h = (x_ref[0] * (1.0 + mod_ref[0, 1:2, :]) + mod_ref[0, 0:1, :]).astype(BF16)
    for si, (start, width, rope, state) in enumerate(segs):
        for c in range(0, width, cw):
            acc = _dot(h, w_ref[:, start + c:start + c + cw])
            if state is not None:
                states[state][0, :, c:c + cw] = acc
            if rope:
                cos, sup, sdn = cos_ref[...], sup_ref[...], sdn_ref[...]
                acc = jnp.concatenate(
                    [_rope_slab(acc[:, s:s + LANES], cos, sup, sdn, rope_n) for s in range(0, cw, LANES)], axis=1)
            outs[si][0, :, c:c + cw] = acc.astype(BF16)


def _project(x, mod, row_of, w, segs, rope=None, tm=512, cw=256, name="proj"):
    B, T, _ = x.shape
    tm = min(tm, T)
    n = w.shape[1]
    n_state = sum(1 for s in segs if s[3] is not None)
    in_specs = [pl.BlockSpec((1, tm, D_MODEL), lambda b, i: (b, i, 0)),
                pl.BlockSpec((1, 6, D_MODEL), lambda b, i: (row_of(b), 0, 0)),
                pl.BlockSpec((D_MODEL, n), lambda b, i: (0, 0))]
    args = [x, mod, w]
    rope_n = 0
    if rope is not None:
        rope_n, tabs = rope
        in_specs += [pl.BlockSpec((tm, LANES), lambda b, i: (i, 0))] * 3
        args += list(tabs)
    out_specs, out_shape = [], []
    for (_, width, _, _) in segs:
        out_specs.append(pl.BlockSpec((1, tm, width), lambda b, i: (b, i, 0)))
        out_shape.append(jax.ShapeDtypeStruct((B, T, width), BF16))
    for (_, width, _, state) in segs:
        if state is not None:
            out_specs.append(pl.BlockSpec((1, tm, width), lambda b, i: (b, i, 0)))
            out_shape.append(jax.ShapeDtypeStruct((B, T, width), F32))
    return pl.pallas_call(
        functools.partial(_proj_kernel, segs=tuple(segs), rope_n=rope_n, n_state=n_state, cw=cw),
        grid=(B, T // tm), in_specs=in_specs, out_specs=out_specs, out_shape=out_shape,
        compiler_params=_cparams(("parallel", "parallel")), name=name,
    )(*args)


def _mla_proj_kernel(*refs, lat, scale, cw):
    x_ref, mod_ref, wa_ref, qn_ref, kvn_ref, wuq_ref, wk_ref, wv_ref = refs[:8]
    pos = 8
    if lat:
        qcos, qsup, qsdn, kcos, ksup, ksdn = refs[8:14]
        pos = 14
    q_out, k_out, v_out = refs[pos:pos + 3]
    h = (x_ref[0] * (1.0 + mod_ref[0, 1:2, :]) + mod_ref[0, 0:1, :]).astype(BF16)
    a = _dot(h, wa_ref[...])
    cq = _rms_norm(a[:, :MLA_Q_RANK], qn_ref[...]).astype(BF16)
    ckv = _rms_norm(a[:, MLA_Q_RANK:MLA_Q_RANK + MLA_KV_RANK], kvn_ref[...])
    kpe = a[:, MLA_Q_RANK + MLA_KV_RANK:]
    if lat:
        kpe = _rope_slab(kpe, kcos[...], ksup[...], ksdn[...], MLA_ROPE // 4)
    else:
        ckv_state, kpe_state = refs[pos + 3:pos + 5]
        ckv_state[0] = ckv
        kpe_state[0] = kpe[:, :MLA_ROPE]
    n_q = wuq_ref.shape[1]
    for c in range(0, n_q, cw):
        q = _dot(cq, wuq_ref[:, c:c + cw]) * scale
        if lat:
            cos, sup, sdn = qcos[...], qsup[...], qsdn[...]
            q = jnp.concatenate(
                [_rope_slab(q[:, s:s + LANES], cos, sup, sdn, MLA_ROPE // 4) for s in range(0, cw, LANES)], axis=1)
        q_out[0, :, c:c + cw] = q.astype(BF16)
    ckv16 = ckv.astype(BF16)
    kin = jnp.concatenate([ckv16, kpe.astype(BF16)], axis=1)
    for c in range(0, n_q, cw):
        k_out[0, :, c:c + cw] = _dot(kin, wk_ref[:, c:c + cw]).astype(BF16)
    n_v = wv_ref.shape[1]
    for c in range(0, n_v, cw):
        v_out[0, :, c:c + cw] = _dot(ckv16, wv_ref[:, c:c + cw]).astype(BF16)


def _mla_project(x, mod, row_of, wa, qn, kvn, wuq, wk, wv, scale, rope=None, tm=512, name="mla_proj"):
    B, T, _ = x.shape
    tm = min(tm, T)
    lat = rope is not None
    const = lambda b, i: (0, 0)
    in_specs = [pl.BlockSpec((1, tm, D_MODEL), lambda b, i: (b, i, 0)),
                pl.BlockSpec((1, 6, D_MODEL), lambda b, i: (row_of(b), 0, 0)),
                pl.BlockSpec(wa.shape, const), pl.BlockSpec(qn.shape, const), pl.BlockSpec(kvn.shape, const),
                pl.BlockSpec(wuq.shape, const), pl.BlockSpec(wk.shape, const), pl.BlockSpec(wv.shape, const)]
    args = [x, mod, wa, qn, kvn, wuq, wk, wv]
    if lat:
        in_specs += [pl.BlockSpec((tm, LANES), lambda b, i: (i, 0))] * 6
        args += list(rope)
    widths = [(wuq.shape[1], BF16), (wk.shape[1], BF16), (wv.shape[1], BF16)]
    if not lat:
        widths += [(MLA_KV_RANK, F32), (MLA_ROPE, F32)]
    out_specs = [pl.BlockSpec((1, tm, w), lambda b, i: (b, i, 0)) for w, _ in widths]
    out_shape = [jax.ShapeDtypeStruct((B, T, w), dt) for w, dt in widths]
    return pl.pallas_call(
        functools.partial(_mla_proj_kernel, lat=lat, scale=scale, cw=512),
        grid=(B, T // tm), in_specs=in_specs, out_specs=out_specs, out_shape=out_shape,
        compiler_params=_cparams(("parallel", "parallel")), name=name,
    )(*args)


def _mla_cache_kernel(ckv_ref, kpe_ref, wk_ref, wv_ref, k_out, v_out):
    ckv16 = ckv_ref[0].astype(BF16)
    kin = jnp.concatenate([ckv16, kpe_ref[0].astype(BF16)], axis=1)
    k_out[0] = _dot(kin, wk_ref[...]).astype(BF16)
    v_out[0] = _dot(ckv16, wv_ref[...]).astype(BF16)


def _mla_cache_expand(c_ckv, c_kpe_slab, wk, wv):
    B, P, _ = c_ckv.shape
    const = lambda b: (0, 0)
    return pl.pallas_call(
        _mla_cache_kernel, grid=(B,),
        in_specs=[pl.BlockSpec((1, P, MLA_KV_RANK), lambda b: (b, 0, 0)),
                  pl.BlockSpec((1, P, LANES), lambda b: (b, 0, 0)),
                  pl.BlockSpec(wk.shape, const), pl.BlockSpec(wv.shape, const)],
        out_specs=[pl.BlockSpec((1, P, wk.shape[1]), lambda b: (b, 0, 0)),
                   pl.BlockSpec((1, P, wv.shape[1]), lambda b: (b, 0, 0))],
        out_shape=[jax.ShapeDtypeStruct((B, P, wk.shape[1]), BF16),
                   jax.ShapeDtypeStruct((B, P, wv.shape[1]), BF16)],
        compiler_params=_cparams(("parallel",)), name="mla_cache_expand",
    )(c_ckv, c_kpe_slab, wk, wv)


def _flash_kernel(*refs, mode, has_cache, n_cache, n_k, tq):
    q_ref = refs[0]
    pos = 1
    if has_cache:
        kc_ref, vc_ref = refs[1:3]
        pos = 3
    ko_ref, vo_ref = refs[pos:pos + 2]
    pos += 2
    if mode == "diff":
        lam_ref, subln_ref = refs[pos:pos + 2]
        pos += 2
    o_ref, m_sc, l_sc, acc_sc = refs[pos:pos + 4]
    j = pl.program_id(3)

    @pl.when(j == 0)
    def _():
        m_sc[...] = jnp.full(m_sc.shape, NEG_INF, F32)
        l_sc[...] = jnp.zeros(l_sc.shape, F32)
        acc_sc[...] = jnp.zeros(acc_sc.shape, F32)

    def step(k_ref, v_ref):
        q = q_ref[0]
        k = k_ref[0]
        if mode == "diff":
            s = _dot_nt(_split_heads(q), k)
        else:
            s = jnp.concatenate([_dot_nt(q[:, :LANES], k[:, :LANES]),
                                 _dot_nt(q[:, LANES:], k[:, LANES:])], axis=0)
        m_prev = m_sc[...]
        m_new = jnp.maximum(m_prev, jnp.max(s, -1, keepdims=True))
        alpha = jnp.exp(m_prev - m_new)
        p = jnp.exp(s - m_new)
        l_sc[...] = alpha * l_sc[...] + jnp.sum(p, -1, keepdims=True)
        acc_sc[...] = alpha * acc_sc[...] + _dot(p.astype(BF16), v_ref[0])
        m_sc[...] = m_new

    if has_cache:
        pl.when(j < n_cache)(lambda: step(kc_ref, vc_ref))
        pl.when(j >= n_cache)(lambda: step(ko_ref, vo_ref))
    else:
        step(ko_ref, vo_ref)

    @pl.when(j == n_k - 1)
    def _():
        o = acc_sc[...] / l_sc[...]
        o0, o1 = o[:tq], o[tq:]
        if mode == "diff":
            lam = lam_ref[...]
            lam_full = (jnp.exp(jnp.sum(lam[0:1] * lam[1:2], keepdims=True))
                        - jnp.exp(jnp.sum(lam[2:3] * lam[3:4], keepdims=True)) + DIFF_LAMBDA_INIT)
            out = _rms_norm(o0 - lam_full * o1, subln_ref[...]) * (1.0 - DIFF_LAMBDA_INIT)
        else:
            out = jnp.where(_lane_lo(o0.shape), o0, o1)
        o_ref[0] = out.astype(o_ref.dtype)


def _flash(q, k, v, cache=None, *, mode, extra=(), tq=256, tk=512, name="flash"):
    B, T, _ = q.shape
    S = k.shape[1]
    wq = LANES if mode == "diff" else 2 * LANES
    G = v.shape[2] // LANES
    tq, tk = min(tq, T), min(tk, S)
    n_own = S // tk
    n_cache = 0
    in_specs = [pl.BlockSpec((1, tq, wq), lambda b, g, i, j: (b, i, g))]
    args = [q]
    if cache is not None:
        kc, vc = cache
        n_cache = kc.shape[1] // tk
        cidx = lambda b, g, i, j: (b, jnp.minimum(j, n_cache - 1), g)
        in_specs += [pl.BlockSpec((1, tk, wq), cidx), pl.BlockSpec((1, tk, LANES), cidx)]
        args += [kc, vc]
    oidx = lambda b, g, i, j: (b, jnp.maximum(j - n_cache, 0), g)
    in_specs += [pl.BlockSpec((1, tk, wq), oidx), pl.BlockSpec((1, tk, LANES), oidx)]
    args += [k, v]
    for e in extra:
        in_specs.append(pl.BlockSpec(e.shape, lambda b, g, i, j: (0, 0)))
        args.append(e)
    n_k = n_cache + n_own
    return pl.pallas_call(
        functools.partial(_flash_kernel, mode=mode, has_cache=cache is not None, n_cache=n_cache, n_k=n_k, tq=tq),
        grid=(B, G, T // tq, n_k), in_specs=in_specs,
        out_specs=pl.BlockSpec((1, tq, LANES), lambda b, g, i, j: (b, i, g)),
        out_shape=jax.ShapeDtypeStruct((B, T, G * LANES), BF16),
        scratch_shapes=[pltpu.VMEM((2 * tq, 1), F32), pltpu.VMEM((2 * tq, 1), F32),
                        pltpu.VMEM((2 * tq, LANES), F32)],
        compiler_params=_cparams(("parallel", "parallel", "parallel", "arbitrary")), name=name,
    )(*args)


def _band_kernel(*refs, n_seg, seg_bias, n_slab, n_bias, has_sink, tq):
    q_ref = refs[0]
    kv = refs[1:1 + 2 * n_seg]
    pos = 1 + 2 * n_seg
    bias_ref = sink_ref = None
    if n_bias:
        bias_ref = refs[pos]
        pos += 1
    if has_sink:
        sink_ref = refs[pos]
        pos += 1
    o_ref = refs[pos]
    bias = None
    if n_bias:
        bias = jnp.concatenate([bias_ref[0, 0], bias_ref[0, n_bias - 1]], axis=0)
    for s in range(n_slab):
        qs = _split_heads(q_ref[0, :, s * LANES:(s + 1) * LANES])
        logits = []
        for i in range(n_seg):
            lg = _dot_nt(qs, kv[2 * i][0])
            if seg_bias[i] is not None:
                lg = lg + bias[:, seg_bias[i]:seg_bias[i] + lg.shape[1]]
            logits.append(lg)
        m = functools.reduce(jnp.maximum, [jnp.max(lg, -1, keepdims=True) for lg in logits])
        if has_sink:
            sk = sink_ref[0, s]
            sink_col = jnp.concatenate([jnp.broadcast_to(sk[:, 0:1], (tq, 1)),
                                        jnp.broadcast_to(sk[:, HALF:HALF + 1], (tq, 1))], axis=0)
            m = jnp.maximum(m, sink_col)
            l = jnp.exp(sink_col - m)
        else:
            l = jnp.zeros_like(m)
        acc = jnp.zeros((2 * tq, LANES), F32)
        for i in range(n_seg):
            p = jnp.exp(logits[i] - m)
            l = l + jnp.sum(p, -1, keepdims=True)
            acc = acc + _dot(p.astype(BF16), kv[2 * i + 1][0])
        o = acc / l
        o_ref[0, :, s * LANES:(s + 1) * LANES] = jnp.where(_lane_lo((tq, LANES)), o[:tq], o[tq:]).astype(o_ref.dtype)


def _band(q, segs, *, n_slab, tq, bias=None, sink=None, name="band"):
    B, T, _ = q.shape
    G = q.shape[2] // (n_slab * LANES)
    nq = T // tq
    in_specs = [pl.BlockSpec((1, tq, n_slab * LANES), lambda g, b, i: (b, i, g))]
    args = [q]
    seg_bias = []
    for (k, v, ts, row_of, boff) in segs:
        idx = (lambda row_of: lambda g, b, i: (b, row_of(i, nq), g))(row_of)
        in_specs += [pl.BlockSpec((1, ts, LANES), idx)] * 2
        args += [k, v]
        seg_bias.append(boff)
    n_bias = 0
    if bias is not None:
        n_var, heads, _, sb = bias.shape
        n_bias = heads // G if heads > 1 else 1
        var_of = lambda i: jnp.where(i == 0, 0, jnp.where(i == nq - 1, n_var - 1, 1))
        in_specs.append(pl.BlockSpec((1, n_bias, tq, sb),
                                     lambda g, b, i: (var_of(i), g if heads > 1 else 0, 0, 0)))
        args.append(bias)
    if sink is not None:
        in_specs.append(pl.BlockSpec((1, n_slab, 1, LANES), lambda g, b, i: (g, 0, 0, 0)))
        args.append(sink)
    return pl.pallas_call(
        functools.partial(_band_kernel, n_seg=len(segs), seg_bias=tuple(seg_bias), n_slab=n_slab,
                          n_bias=n_bias, has_sink=sink is not None, tq=tq),
        grid=(G, B, nq), in_specs=in_specs,
        out_specs=pl.BlockSpec((1, tq, n_slab * LANES), lambda g, b, i: (b, i, g)),
        out_shape=jax.ShapeDtypeStruct(q.shape, BF16),
        compiler_params=_cparams(("parallel", "parallel", "parallel")), name=name,
    )(*args)


_OWN = lambda i, nq: i
_PREV = lambda i, nq: jnp.maximum(i - 1, 0)
_NEXT = lambda i, nq: jnp.minimum(i + 1, nq - 1)
_FIRST = lambda i, nq: 0


def _post_kernel(x_ref, o_ref, mod_ref, g_ref, b_ref, wo_ref, w1_ref, w2_ref, out_ref, *, cf):
    x = x_ref[0]
    y = _dot(o_ref[0], wo_ref[...])
    x1 = _layer_norm(ALPHA * x + mod_ref[0, 2:3, :] * y, g_ref[0:1, :], b_ref[0:1, :])
    h = (x1 * (1.0 + mod_ref[0, 4:5, :]) + mod_ref[0, 3:4, :]).astype(BF16)
    acc = jnp.zeros(x.shape, F32)
    for c in range(0, D_FF, cf):
        u = jnp.maximum(_dot(h, w1_ref[:, c:c + cf]), 0.0)
        acc = acc + _dot((u * u).astype(BF16), w2_ref[c:c + cf, :])
    out_ref[0] = _layer_norm(ALPHA * x1 + mod_ref[0, 5:6, :] * acc, g_ref[1:2, :], b_ref[1:2, :])


def _post(x, o, mod, row_of, ln_g, ln_b, wo, w1, w2, tm=256, name="post"):
    B, T, _ = x.shape
    tm = min(tm, T)
    const = lambda b, i: (0, 0)
    once = pl.Buffered(1)
    return pl.pallas_call(
        functools.partial(_post_kernel, cf=1024),
        grid=(B, T // tm),
        in_specs=[pl.BlockSpec((1, tm, D_MODEL), lambda b, i: (b, i, 0)),
                  pl.BlockSpec((1, tm, D_MODEL), lambda b, i: (b, i, 0)),
                  pl.BlockSpec((1, 6, D_MODEL), lambda b, i: (row_of(b), 0, 0)),
                  pl.BlockSpec((2, D_MODEL), const), pl.BlockSpec((2, D_MODEL), const),
                  pl.BlockSpec(wo.shape, const, pipeline_mode=once),
                  pl.BlockSpec(w1.shape, const, pipeline_mode=once),
                  pl.BlockSpec(w2.shape, const, pipeline_mode=once)],
        out_specs=pl.BlockSpec((1, tm, D_MODEL), lambda b, i: (b, i, 0)),
        out_shape=jax.ShapeDtypeStruct(x.shape, F32),
        compiler_params=_cparams(("parallel", "parallel")), name=name,
    )(x, o, mod, ln_g, ln_b, wo, w1, w2)


def _rope_tables(T, R, lead, reps):
    n = R // 4
    t = jnp.arange(T)
    inv = ROPE_BASE ** (-jnp.arange(n, dtype=jnp.float32) / n)

    def cs(p):
        ang = p.astype(jnp.float32)[:, None] * inv
        return jnp.cos(ang), jnp.sin(ang)

    cr, sr = cs(t // GRID_W)
    cc, sc = cs(t % GRID_W)
    z = jnp.zeros_like(sr)
    cos = jnp.concatenate([cr, cr, cc, cc], -1)
    sup = jnp.concatenate([-sr, z, -sc, z], -1)
    sdn = jnp.concatenate([z, sr, z, sc], -1)
    tail = LANES - lead - reps * R

    def slab(a, fill):
        parts = [jnp.full((T, lead), fill, F32)] if lead else []
        parts += [a] * reps
        if tail:
            parts.append(jnp.full((T, tail), fill, F32))
        return jnp.concatenate(parts, -1)

    return slab(cos, 1.0), slab(sup, 0.0), slab(sdn, 0.0)


def _na_bias_tables(rpb, rows):
    qr, kr_n = 4, 12
    tabs = []
    for r_base in (0, qr, rows - qr):
        r = r_base + np.arange(qr)[:, None, None, None]
        c = np.arange(GRID_W)[None, :, None, None]
        kr = r_base - qr + np.arange(kr_n)[None, None, :, None]
        kc = np.arange(GRID_W)[None, None, None, :]
        r0 = np.clip(r - NA_KH // 2, 0, rows - NA_KH)
        c0 = np.clip(c - NA_KW // 2, 0, GRID_W - NA_KW)
        valid = (kr >= r0) & (kr < r0 + NA_KH) & (kc >= c0) & (kc < c0 + NA_KW)
        row_rel = np.clip(kr - r + NA_KH - 1, 0, 2 * NA_KH - 2) + 0 * kc + 0 * c
        col_rel = np.clip(kc - c + NA_KW - 1, 0, 2 * NA_KW - 2) + 0 * kr + 0 * r
        shape = (qr * GRID_W, kr_n * GRID_W)
        valid = np.broadcast_to(valid, (qr, GRID_W, kr_n, GRID_W)).reshape(shape)
        vals = rpb.astype(F32)[:, row_rel.reshape(shape), col_rel.reshape(shape)]
        tabs.append(jnp.where(valid[None], vals, NEG_INF))
    return jnp.stack(tabs, 0)


def _swa_mask_tables(nb):
    qi = np.arange(SWA_QB)[:, None]
    kj = np.arange(3 * SWA_QB)[None, :]
    tabs = []
    for blk in (0, 1, nb - 1):
        kpos = (blk - 1) * SWA_QB + kj
        ok = (np.abs(kj - SWA_QB - qi) <= SWA_WINDOW) & (kpos >= 0) & (kpos < nb * SWA_QB)
        tabs.append(np.where(ok, 0.0, NEG_INF).astype(np.float32)[None])
    return jnp.asarray(np.stack(tabs, 0))


_SWA_PERM = np.array([8 * p + 4 * half + s for p in range(2) for s in range(4) for half in range(2)])


def _perm_heads(w, axis, dh=HALF):
    shape = w.shape
    w = w.reshape(shape[:axis] + (SWA_HEADS, dh) + shape[axis + 1:])
    w = jnp.take(w, jnp.asarray(_SWA_PERM), axis=axis)
    return w.reshape(shape)


def kernel(x_prompt, x_sample, cache_l0_k, cache_l0_v, cache_l1_k, cache_l1_v, cache_l2_ckv, cache_l2_kpe, cache_l3_k, cache_l3_v, c, c_ctx, w_mod, b_mod, ln_g, ln_b, w_mlp1, w_mlp2, l0_w_qkv, l0_lam, l0_subln, l0_w_o, l1_w_qkv, l1_rpb, l1_w_o, l2_w_a, l2_q_norm, l2_kv_norm, l2_w_uq, l2_w_ukv, l2_w_o, l3_w_qkv, l3_sink, l3_w_o):
    D = D_MODEL
    Bc, S, _ = x_prompt.shape
    Bl, T, _ = x_sample.shape
    P = cache_l0_k.shape[1]
    assert 1 + Bl <= MOD_ROWS and T % GRID_W == 0
    rows = T // GRID_W

    cond = jnp.concatenate([c_ctx[None], c, jnp.zeros((MOD_ROWS - 1 - Bl, D), F32)], 0)
    mod = _modulation(cond, w_mod, b_mod)
    ctx_row = lambda b: 0
    lat_row = lambda b: b + 1

    bf = lambda a: a.astype(BF16)
    qscale = lambda w, nq, s: jnp.concatenate([w[:, :nq] * s, w[:, nq:]], 1)
    w1 = bf(w_mlp1)
    w2 = bf(w_mlp2)

    rope64 = (DIFF_DH // 4, _rope_tables(T, DIFF_DH, 0, 2))

    w0 = bf(qscale(l0_w_qkv, D, DIFF_DH ** -0.5))
    wo0 = bf(l0_w_o)
    lam = l0_lam.astype(F32)
    subln = l0_subln.reshape(1, 2 * DIFF_DH)
    segs_c = [(0, D, False, None), (D, D, False, 0), (2 * D, D, False, 1)]
    segs_l = [(0, D, True, None), (D, D, True, None), (2 * D, D, False, None)]
    q, k, v, k0, v0 = _project(x_prompt, mod[0], ctx_row, w0, segs_c, name="l0_proj_ctx")
    o = _flash(q, k, v, mode="diff", extra=(lam, subln), name="l0_attn_ctx")
    xp = _post(x_prompt, o, mod[0], ctx_row, ln_g[0], ln_b[0], wo0, w1[0], w2[0], name="l0_post_ctx")
    q, k, v = _project(x_sample, mod[0], lat_row, w0, segs_l, rope=rope64, name="l0_proj_lat")
    cache = (bf(cache_l0_k.reshape(Bl, P, D)), bf(cache_l0_v.reshape(Bl, P, D)))
    o = _flash(q, k, v, cache, mode="diff", extra=(lam, subln), name="l0_attn_lat")
    xs = _post(x_sample, o, mod[0], lat_row, ln_g[0], ln_b[0], wo0, w1[0], w2[0], name="l0_post_lat")

    w = bf(qscale(l1_w_qkv, D, 64 ** -0.5))
    wo = bf(l1_w_o)
    q, k, v, k1, v1 = _project(xp, mod[1], ctx_row, w, segs_c, name="l1_proj_ctx")
    o = _band(q, [(k, v, S, _OWN, None)], n_slab=1, tq=S, name="l1_attn_ctx")
    xp = _post(xp, o, mod[1], ctx_row, ln_g[1], ln_b[1], wo, w1[1], w2[1], name="l1_post_ctx")
    segs_n = [(0, D, False, None), (D, D, False, None), (2 * D, D, False, None)]
    q, k, v = _project(xs, mod[1], lat_row, w, segs_n, name="l1_proj_lat")
    kc, vc = bf(cache_l1_k.reshape(Bl, P, D)), bf(cache_l1_v.reshape(Bl, P, D))
    tqn = 4 * GRID_W
    bias = _na_bias_tables(l1_rpb, rows)
    o = _band(q, [(k, v, tqn, _PREV, 0), (k, v, tqn, _OWN, tqn), (k, v, tqn, _NEXT, 2 * tqn),
                  (kc, vc, P, _FIRST, None)], n_slab=1, tq=tqn, bias=bias, name="l1_attn_lat")
    xs = _post(xs, o, mod[1], lat_row, ln_g[1], ln_b[1], wo, w1[1], w2[1], name="l1_post_lat")

    H, dq = MLA_HEADS, MLA_NOPE + MLA_ROPE
    wa = bf(jnp.pad(l2_w_a, ((0, 0), (0, 7 * LANES - l2_w_a.shape[1]))))
    wuq = bf(jnp.pad(l2_w_uq.reshape(MLA_Q_RANK, H, dq), ((0, 0), (0, 0), (0, LANES - dq))).reshape(MLA_Q_RANK, H * LANES))
    wukv = l2_w_ukv.reshape(MLA_KV_RANK, H, MLA_NOPE + MLA_VDIM)
    wk_n = jnp.pad(wukv[:, :, :MLA_NOPE], ((0, 0), (0, 0), (0, LANES - MLA_NOPE))).reshape(MLA_KV_RANK, H * LANES)
    place = np.zeros((LANES, H, LANES), np.float32)
    place[np.arange(MLA_ROPE), :, MLA_NOPE + np.arange(MLA_ROPE)] = 1.0
    wk = bf(jnp.concatenate([wk_n, jnp.asarray(place.reshape(LANES, H * LANES))], 0))
    wv = bf(wukv[:, :, MLA_NOPE:].reshape(MLA_KV_RANK, H * MLA_VDIM))
    wo = bf(l2_w_o)
    qn, kvn = l2_q_norm.reshape(1, -1), l2_kv_norm.reshape(1, -1)
    mscale = dq ** -0.5
    q, k, v, ckv2, kpe2 = _mla_project(xp, mod[2], ctx_row, wa, qn, kvn, wuq, wk, wv, mscale, name="l2_proj_ctx")
    o = _flash(q, k, v, mode="mla", name="l2_attn_ctx")
    xp = _post(xp, o, mod[2], ctx_row, ln_g[2], ln_b[2], wo, w1[2], w2[2], name="l2_post_ctx")
    tabs = _rope_tables(T, MLA_ROPE, MLA_NOPE, 1) + _rope_tables(T, MLA_ROPE, 0, 1)
    q, k, v = _mla_project(xs, mod[2], lat_row, wa, qn, kvn, wuq, wk, wv, mscale, rope=tabs, name="l2_proj_lat")
    kpe_slab = jnp.pad(cache_l2_kpe, ((0, 0), (0, 0), (0, LANES - MLA_ROPE)))
    cache = _mla_cache_expand(cache_l2_ckv, kpe_slab, wk, wv)
    o = _flash(q, k, v, cache, mode="mla", name="l2_attn_lat")
    xs = _post(xs, o, mod[2], lat_row, ln_g[2], ln_b[2], wo, w1[2], w2[2], name="l2_post_lat")

    nq3, nk3 = SWA_HEADS * HALF, SWA_KV_HEADS * HALF
    w = jnp.concatenate([_perm_heads(l3_w_qkv[:, :nq3], 1) * (HALF ** -0.5), l3_w_qkv[:, nq3:]], 1)
    w = bf(w)
    wo = bf(_perm_heads(l3_w_o, 0))
    sink = jnp.take(l3_sink.astype(F32), jnp.asarray(_SWA_PERM))
    sink = jnp.repeat(sink.reshape(2, 4, 1, 2), HALF, axis=-1)
    segs_c = [(0, nq3, False, None), (nq3, nk3, False, 0), (nq3 + nk3, nk3, False, 1)]
    segs_l = [(0, nq3, True, None), (nq3, nk3, True, None), (nq3 + nk3, nk3, False, None)]
    q, k, v, k3, v3 = _project(xp, mod[3], ctx_row, w, segs_c, name="l3_proj_ctx")
    o = _band(q, [(k, v, S, _OWN, None)], n_slab=4, tq=S, sink=sink, name="l3_attn_ctx")
    xp = _post(xp, o, mod[3], ctx_row, ln_g[3], ln_b[3], wo, w1[3], w2[3], name="l3_post_ctx")
    q, k, v = _project(xs, mod[3], lat_row, w, segs_l, rope=rope64, name="l3_proj_lat")
    kc, vc = bf(cache_l3_k.reshape(Bl, P, nk3)), bf(cache_l3_v.reshape(Bl, P, nk3))
    mask = _swa_mask_tables(T // SWA_QB)
    o = _band(q, [(k, v, SWA_QB, _PREV, 0), (k, v, SWA_QB, _OWN, SWA_QB), (k, v, SWA_QB, _NEXT, 2 * SWA_QB),
                  (kc, vc, P, _FIRST, None)], n_slab=4, tq=SWA_QB, bias=mask, sink=sink, name="l3_attn_lat")
    xs = _post(xs, o, mod[3], lat_row, ln_g[3], ln_b[3], wo, w1[3], w2[3], name="l3_post_lat")

    return (xp, xs,
            k0.reshape(Bc, S, DIFF_HEADS, 2 * DIFF_DH), v0.reshape(Bc, S, DIFF_HEADS, 2 * DIFF_DH),
            k1.reshape(Bc, S, NA_HEADS, HALF), v1.reshape(Bc, S, NA_HEADS, HALF),
            ckv2, kpe2,
            k3.reshape(Bc, S, SWA_KV_HEADS, HALF), v3.reshape(Bc, S, SWA_KV_HEADS, HALF))
```

```python
import functools
import math

import numpy as np
import jax
import jax.numpy as jnp
from jax import lax
from jax.experimental import pallas as pl
from jax.experimental.pallas import tpu as pltpu

D_MODEL = 1024
DEPTH = 4
GRID_W = 64
ROPE_BASE = 10000.0
LN_EPS = 1e-5
NEG_INF = -1e30
ALPHA = (2 * DEPTH) ** 0.25
D_FF = 4 * D_MODEL
DIFF_DH = 64
DIFF_HEADS = 8
DIFF_LAMBDA_INIT = 0.8 - 0.6 * math.exp(-0.3 * 0)
NA_HEADS = 16
NA_KH = 8
NA_KW = 16
MLA_HEADS = 16
MLA_NOPE = 64
MLA_ROPE = 32
MLA_VDIM = 64
MLA_Q_RANK = 512
MLA_KV_RANK = 256
SWA_HEADS = 16
SWA_KV_HEADS = 4
SWA_WINDOW = 128
SWA_QB = 128

LANES = 128
HALF = LANES // 2
MOD_ROWS = 16
VMEM_LIMIT = 56 * 1024 * 1024

F32 = jnp.float32
BF16 = jnp.bfloat16


def _cparams(sem):
    return pltpu.CompilerParams(dimension_semantics=sem, vmem_limit_bytes=VMEM_LIMIT)


def _dot(a, b):
    return jnp.dot(a, b, preferred_element_type=F32)


def _dot_nt(a, b):
    return lax.dot_general(a, b, (((1,), (1,)), ((), ())), preferred_element_type=F32)


def _layer_norm(x, g, b):
    mu = jnp.mean(x, -1, keepdims=True)
    xc = x - mu
    var = jnp.mean(xc * xc, -1, keepdims=True)
    return xc * lax.rsqrt(var + LN_EPS) * g + b


def _rms_norm(x, g, axis=-1):
    return x * lax.rsqrt(jnp.mean(x * x, axis, keepdims=True) + LN_EPS) * g


def _rope_slab(x, cos, sin_up, sin_dn, n):
    return x * cos + pltpu.roll(x, LANES - n, 1) * sin_up + pltpu.roll(x, n, 1) * sin_dn


def _rope_slab_t(x, cos, sin_up, sin_dn, n):
    up = jnp.concatenate([x[n:], x[:n]], axis=0)
    dn = jnp.concatenate([x[LANES - n:], x[:LANES - n]], axis=0)
    return x * cos + up * sin_up + dn * sin_dn


def _lane_lo(shape):
    return lax.broadcasted_iota(jnp.int32, shape, len(shape) - 1) < HALF


def _split_heads(q):
    lo = _lane_lo(q.shape)
    zero = jnp.zeros_like(q)
    return jnp.concatenate([jnp.where(lo, q, zero), jnp.where(lo, zero, q)], axis=0)


def _modulated(x_ref, mod_ref):
    return x_ref[0] * (1.0 + mod_ref[0, 1:2, :]) + mod_ref[0, 0:1, :]


def _mod_kernel(c_ref, w_ref, b_ref, o_ref):
    c = c_ref[...]
    s = c / (1.0 + jnp.exp(-c))
    o_ref[0] = _dot(s.astype(BF16), w_ref[0].astype(BF16)) + b_ref[0]


def _modulation(cond, w_mod, b_mod):
    tn = 512
    n = 6 * D_MODEL
    out = pl.pallas_call(
        _mod_kernel,
        grid=(DEPTH, n // tn),
        in_specs=[pl.BlockSpec((MOD_ROWS, D_MODEL), lambda l, j: (0, 0)),
                  pl.BlockSpec((1, D_MODEL, tn), lambda l, j: (l, 0, j)),
                  pl.BlockSpec((1, 1, tn), lambda l, j: (l, 0, j))],
        out_specs=pl.BlockSpec((1, MOD_ROWS, tn), lambda l, j: (l, 0, j)),
        out_shape=jax.ShapeDtypeStruct((DEPTH, MOD_ROWS, n), F32),
        compiler_params=_cparams(("parallel", "parallel")),
        name="modulation",
    )(cond, w_mod, b_mod.reshape(DEPTH, 1, n))
    return out.reshape(DEPTH, MOD_ROWS, 6, D_MODEL)


def _proj_kernel(*refs, segs, tsegs, rope_n, cw):
    x_ref, mod_ref = refs[:2]
    pos = 2
    w_ref = wt_ref = None
    if segs:
        w_ref = refs[pos]
        pos += 1
    if tsegs:
        wt_ref = refs[pos]
        pos += 1
    if rope_n and any(s[2] for s in segs):
        cos, sup, sdn = (r[...] for r in refs[pos:pos + 3])
        pos += 3
    if rope_n and any(s[2] for s in tsegs):
        cos_t, sup_t, sdn_t = (r[...] for r in refs[pos:pos + 3])
        pos += 3
    outs = list(refs[pos:])
    h32 = _modulated(x_ref, mod_ref)
    h = h32.astype(BF16)
    for (start, width, rope, want16, want32) in segs:
        o16 = outs.pop(0) if want16 else None
        o32 = outs.pop(0) if want32 else None
        for c in range(0, width, cw):
            acc = _dot(h, w_ref[:, start + c:start + c + cw])
            if want32:
                o32[0, :, c:c + cw] = acc
            if rope:
                acc = jnp.concatenate(
                    [_rope_slab(acc[:, s:s + LANES], cos, sup, sdn, rope_n) for s in range(0, cw, LANES)], axis=1)
            if want16:
                o16[0, :, c:c + cw] = acc.astype(BF16)
    if tsegs:
        ht = h32.T.astype(BF16)
        for (start, width, rope, _, _) in tsegs:
            o16 = outs.pop(0)
            for c in range(0, width, cw):
                acc = _dot(wt_ref[start + c:start + c + cw, :], ht)
                if rope:
                    acc = jnp.concatenate(
                        [_rope_slab_t(acc[s:s + LANES], cos_t, sup_t, sdn_t, rope_n) for s in range(0, cw, LANES)],
                        axis=0)
                o16[0, c:c + cw, :] = acc.astype(BF16)


def _project(x, mod, row_of, w, wt, segs, tsegs, rope=None, rope_t=None, tm=512, cw=256, name="proj"):
    B, T, _ = x.shape
    tm = min(tm, T)
    const = lambda b, i: (0, 0)
    in_specs = [pl.BlockSpec((1, tm, D_MODEL), lambda b, i: (b, i, 0)),
                pl.BlockSpec((1, 6, D_MODEL), lambda b, i: (row_of(b), 0, 0))]
    args = [x, mod]
    if segs:
        in_specs.append(pl.BlockSpec(w.shape, const))
        args.append(w)
    if tsegs:
        in_specs.append(pl.BlockSpec(wt.shape, const))
        args.append(wt)
    rope_n = 0
    if rope is not None:
        rope_n, tabs = rope
        in_specs += [pl.BlockSpec((tm, LANES), lambda b, i: (i, 0))] * 3
        args += list(tabs)
    if rope_t is not None:
        rope_n, tabs = rope_t
        in_specs += [pl.BlockSpec((LANES, tm), lambda b, i: (0, i))] * 3
        args += list(tabs)
    out_specs, out_shape = [], []
    for (_, width, _, want16, want32) in segs:
        for want, dt in ((want16, BF16), (want32, F32)):
            if want:
                out_specs.append(pl.BlockSpec((1, tm, width), lambda b, i: (b, i, 0)))
                out_shape.append(jax.ShapeDtypeStruct((B, T, width), dt))
    for (_, width, _, _, _) in tsegs:
        out_specs.append(pl.BlockSpec((1, width, tm), lambda b, i: (b, 0, i)))
        out_shape.append(jax.ShapeDtypeStruct((B, width, T), BF16))
    return pl.pallas_call(
        functools.partial(_proj_kernel, segs=tuple(segs), tsegs=tuple(tsegs), rope_n=rope_n, cw=cw),
        grid=(B, T // tm), in_specs=in_specs, out_specs=out_specs, out_shape=out_shape,
        compiler_params=_cparams(("parallel", "parallel")), name=name,
    )(*args)


def _mla_proj_kernel(*refs, lat, scale, cw):
    x_ref, mod_ref, wa_ref, qn_ref, kvn_ref, wuqt_ref, wk_ref, wvt_ref = refs[:8]
    pos = 8
    if lat:
        qcos, qsup, qsdn, kcos, ksup, ksdn = (r[...] for r in refs[8:14])
        pos = 14
    qt_out, k_out, vt_out = refs[pos:pos + 3]
    h = _modulated(x_ref, mod_ref).astype(BF16)
    a = _dot(h, wa_ref[...])
    cq = _rms_norm(a[:, :MLA_Q_RANK], qn_ref[...])
    ckv = _rms_norm(a[:, MLA_Q_RANK:MLA_Q_RANK + MLA_KV_RANK], kvn_ref[...])
    kpe = a[:, MLA_Q_RANK + MLA_KV_RANK:]
    if lat:
        kpe = _rope_slab(kpe, kcos, ksup, ksdn, MLA_ROPE // 4)
    else:
        ckv_state, kpe_state = refs[pos + 3:pos + 5]
        ckv_state[0] = ckv
        kpe_state[0] = kpe[:, :MLA_ROPE]
    cqt = cq.T.astype(BF16)
    n_q = wuqt_ref.shape[0]
    for c in range(0, n_q, cw):
        q = _dot(wuqt_ref[c:c + cw, :], cqt) * scale
        if lat:
            q = jnp.concatenate(
                [_rope_slab_t(q[s:s + LANES], qcos, qsup, qsdn, MLA_ROPE // 4) for s in range(0, cw, LANES)], axis=0)
        qt_out[0, c:c + cw, :] = q.astype(BF16)
    kin = jnp.concatenate([ckv.astype(BF16), kpe.astype(BF16)], axis=1)
    for c in range(0, wk_ref.shape[1], cw):
        k_out[0, :, c:c + cw] = _dot(kin, wk_ref[:, c:c + cw]).astype(BF16)
    ckvt = ckv.T.astype(BF16)
    for c in range(0, wvt_ref.shape[0], cw):
        vt_out[0, c:c + cw, :] = _dot(wvt_ref[c:c + cw, :], ckvt).astype(BF16)


def _mla_project(x, mod, row_of, wa, qn, kvn, wuqt, wk, wvt, scale, rope=None, tm=512, name="mla_proj"):
    B, T, _ = x.shape
    tm = min(tm, T)
    lat = rope is not None
    const = lambda b, i: (0, 0)
    in_specs = [pl.BlockSpec((1, tm, D_MODEL), lambda b, i: (b, i, 0)),
                pl.BlockSpec((1, 6, D_MODEL), lambda b, i: (row_of(b), 0, 0))]
    in_specs += [pl.BlockSpec(a.shape, const) for a in (wa, qn, kvn, wuqt, wk, wvt)]
    args = [x, mod, wa, qn, kvn, wuqt, wk, wvt]
    if lat:
        in_specs += [pl.BlockSpec((LANES, tm), lambda b, i: (0, i))] * 3
        in_specs += [pl.BlockSpec((tm, LANES), lambda b, i: (i, 0))] * 3
        args += list(rope)
    tok = lambda b, i: (b, i, 0)
    feat = lambda b, i: (b, 0, i)
    out_specs = [pl.BlockSpec((1, wuqt.shape[0], tm), feat), pl.BlockSpec((1, tm, wk.shape[1]), tok),
                 pl.BlockSpec((1, wvt.shape[0], tm), feat)]
    out_shape = [jax.ShapeDtypeStruct((B, wuqt.shape[0], T), BF16), jax.ShapeDtypeStruct((B, T, wk.shape[1]), BF16),
                 jax.ShapeDtypeStruct((B, wvt.shape[0], T), BF16)]
    if not lat:
        out_specs += [pl.BlockSpec((1, tm, MLA_KV_RANK), tok), pl.BlockSpec((1, tm, MLA_ROPE), tok)]
        out_shape += [jax.ShapeDtypeStruct((B, T, MLA_KV_RANK), F32), jax.ShapeDtypeStruct((B, T, MLA_ROPE), F32)]
    return pl.pallas_call(
        functools.partial(_mla_proj_kernel, lat=lat, scale=scale, cw=512),
        grid=(B, T // tm), in_specs=in_specs, out_specs=out_specs, out_shape=out_shape,
        compiler_params=_cparams(("parallel", "parallel")), name=name,
    )(*args)


def _mla_cache_kernel(ckv_ref, kpe_ref, wk_ref, wvt_ref, k_out, vt_out):
    ckv = ckv_ref[0]
    kin = jnp.concatenate([ckv.astype(BF16), kpe_ref[0].astype(BF16)], axis=1)
    k_out[0] = _dot(kin, wk_ref[...]).astype(BF16)
    vt_out[0] = _dot(wvt_ref[...], ckv.T.astype(BF16)).astype(BF16)


def _mla_cache_expand(c_ckv, c_kpe_slab, wk, wvt):
    B, P, _ = c_ckv.shape
    const = lambda b: (0, 0)
    return pl.pallas_call(
        _mla_cache_kernel, grid=(B,),
        in_specs=[pl.BlockSpec((1, P, MLA_KV_RANK), lambda b: (b, 0, 0)),
                  pl.BlockSpec((1, P, LANES), lambda b: (b, 0, 0)),
                  pl.BlockSpec(wk.shape, const), pl.BlockSpec(wvt.shape, const)],
        out_specs=[pl.BlockSpec((1, P, wk.shape[1]), lambda b: (b, 0, 0)),
                   pl.BlockSpec((1, wvt.shape[0], P), lambda b: (b, 0, 0))],
        out_shape=[jax.ShapeDtypeStruct((B, P, wk.shape[1]), BF16),
                   jax.ShapeDtypeStruct((B, wvt.shape[0], P), BF16)],
        compiler_params=_cparams(("parallel",)), name="mla_cache_expand",
    )(c_ckv, c_kpe_slab, wk, wvt)


def _flash_kernel(*refs, mode, has_cache, tq, tk):
    qt_ref = refs[0]
    pos = 1
    if has_cache:
        kc_ref, vct_ref = refs[1:3]
        pos = 3
    k_ref, vt_ref = refs[pos:pos + 2]
    pos += 2
    if mode == "diff":
        lam_ref, subln_ref = refs[pos:pos + 2]
        pos += 2
    o_ref, acc_sc = refs[pos:pos + 2]

    qt = qt_ref[0]
    if mode == "diff":
        lo = lax.broadcasted_iota(jnp.int32, qt.shape, 0) < HALF
        zero = jnp.zeros_like(qt)
        qst = jnp.concatenate([jnp.where(lo, qt, zero), jnp.where(lo, zero, qt)], axis=1)

    def block(k, vt, m, l):
        if mode == "diff":
            s = _dot(k, qst)
        else:
            s = jnp.concatenate([_dot(k[:, :LANES], qt[:LANES]), _dot(k[:, LANES:], qt[LANES:])], axis=1)
        m_cur = jnp.max(s, 0, keepdims=True)
        m_new = m_cur if m is None else jnp.maximum(m, m_cur)
        p = jnp.exp(s - m_new)
        l_cur = jnp.sum(p, 0, keepdims=True)
        pb = p.astype(BF16)
        if mode == "diff":
            pv = _dot(vt, pb)
        else:
            pv = jnp.concatenate([_dot(vt[:HALF], pb[:, :tq]), _dot(vt[HALF:], pb[:, tq:])], axis=1)
        if m is None:
            acc_sc[...] = pv
            return m_new, l_cur
        alpha = jnp.exp(m - m_new)
        acc_sc[...] = alpha * acc_sc[...] + pv
        return m_new, alpha * l + l_cur

    m = l = None
    if has_cache:
        for j in range(kc_ref.shape[1] // tk):
            m, l = block(kc_ref[0, j * tk:(j + 1) * tk, :], vct_ref[0, :, j * tk:(j + 1) * tk], m, l)
    for j in range(k_ref.shape[1] // tk):
        m, l = block(k_ref[0, j * tk:(j + 1) * tk, :], vt_ref[0, :, j * tk:(j + 1) * tk], m, l)

    o = acc_sc[...] / l
    o0, o1 = o[:, :tq], o[:, tq:]
    if mode == "diff":
        lam = lam_ref[...]
        lam_full = (jnp.exp(jnp.sum(lam[0:1] * lam[1:2], keepdims=True))
                    - jnp.exp(jnp.sum(lam[2:3] * lam[3:4], keepdims=True)) + DIFF_LAMBDA_INIT)
        out_t = _rms_norm(o0 - lam_full * o1, subln_ref[...], axis=0) * (1.0 - DIFF_LAMBDA_INIT)
    else:
        out_t = jnp.concatenate([o0, o1], axis=0)
    o_ref[0] = out_t.T.astype(o_ref.dtype)


def _flash(qt, k, vt, cache=None, *, mode, extra=(), tq=256, tk=512, name="flash"):
    B, _, T = qt.shape
    S = k.shape[1]
    wq = LANES if mode == "diff" else 2 * LANES
    G = vt.shape[1] // LANES
    tq, tk = min(tq, T), min(tk, S)
    in_specs = [pl.BlockSpec((1, wq, tq), lambda b, g, i: (b, g, i))]
    args = [qt]
    if cache is not None:
        kc, vct = cache
        P = kc.shape[1]
        in_specs += [pl.BlockSpec((1, P, wq), lambda b, g, i: (b, 0, g)),
                     pl.BlockSpec((1, LANES, P), lambda b, g, i: (b, g, 0))]
        args += [kc, vct]
    in_specs += [pl.BlockSpec((1, S, wq), lambda b, g, i: (b, 0, g)),
                 pl.BlockSpec((1, LANES, S), lambda b, g, i: (b, g, 0))]
    args += [k, vt]
    for e in extra:
        in_specs.append(pl.BlockSpec(e.shape, lambda b, g, i: (0, 0)))
        args.append(e)
    dv = LANES if mode == "diff" else HALF
    return pl.pallas_call(
        functools.partial(_flash_kernel, mode=mode, has_cache=cache is not None, tq=tq, tk=tk),
        grid=(B, G, T // tq), in_specs=in_specs,
        out_specs=pl.BlockSpec((1, tq, LANES), lambda b, g, i: (b, i, g)),
        out_shape=jax.ShapeDtypeStruct((B, T, G * LANES), BF16),
        scratch_shapes=[pltpu.VMEM((dv, 2 * tq), F32)],
        compiler_params=_cparams(("parallel", "parallel", "parallel")), name=name,
    )(*args)


def _band_kernel(*refs, n_seg, seg_bias, n_slab, n_bias, has_sink, tq):
    q_ref = refs[0]
    kv = refs[1:1 + 2 * n_seg]
    pos = 1 + 2 * n_seg
    bias_ref = sink_ref = None
    if n_bias:
        bias_ref = refs[pos]
        pos += 1
    if has_sink:
        sink_ref = refs[pos]
        pos += 1
    o_ref = refs[pos]
    bias = None
    if n_bias:
        bias = jnp.concatenate([bias_ref[0, 0], bias_ref[0, n_bias - 1]], axis=0)
    for s in range(n_slab):
        qs = _split_heads(q_ref[0, :, s * LANES:(s + 1) * LANES])
        logits = []
        for i in range(n_seg):
            lg = _dot_nt(qs, kv[2 * i][0])
            if seg_bias[i] is not None:
                lg = lg + bias[:, seg_bias[i]:seg_bias[i] + lg.shape[1]]
            logits.append(lg)
        m = functools.reduce(jnp.maximum, [jnp.max(lg, -1, keepdims=True) for lg in logits])
        if has_sink:
            sk = sink_ref[0, s]
            sink_col = jnp.concatenate([jnp.broadcast_to(sk[:, 0:1], (tq, 1)),
                                        jnp.broadcast_to(sk[:, HALF:HALF + 1], (tq, 1))], axis=0)
            m = jnp.maximum(m, sink_col)
            l = jnp.exp(sink_col - m)
        else:
            l = jnp.zeros_like(m)
        acc = jnp.zeros((2 * tq, LANES), F32)
        for i in range(n_seg):
            p = jnp.exp(logits[i] - m)
            l = l + jnp.sum(p, -1, keepdims=True)
            acc = acc + _dot(p.astype(BF16), kv[2 * i + 1][0])
        o = acc / l
        o_ref[0, :, s * LANES:(s + 1) * LANES] = jnp.where(_lane_lo((tq, LANES)), o[:tq], o[tq:]).astype(o_ref.dtype)


def _band(q, segs, *, n_slab, tq, bias=None, sink=None, name="band"):
    B, T, _ = q.shape
    G = q.shape[2] // (n_slab * LANES)
    nq = T // tq
    in_specs = [pl.BlockSpec((1, tq, n_slab * LANES), lambda g, b, i: (b, i, g))]
    args = [q]
    seg_bias = []
    for (k, v, ts, row_of, boff) in segs:
        idx = (lambda row_of: lambda g, b, i: (b, row_of(i, nq), g))(row_of)
        in_specs += [pl.BlockSpec((1, ts, LANES), idx)] * 2
        args += [k, v]
        seg_bias.append(boff)
    n_bias = 0
    if bias is not None:
        n_var, heads, _, sb = bias.shape
        n_bias = heads // G if heads > 1 else 1
        var_of = lambda i: jnp.where(i == 0, 0, jnp.where(i == nq - 1, n_var - 1, 1))
        in_specs.append(pl.BlockSpec((1, n_bias, tq, sb),
                                     lambda g, b, i: (var_of(i), g if heads > 1 else 0, 0, 0)))
        args.append(bias)
    if sink is not None:
        in_specs.append(pl.BlockSpec((1, n_slab, 1, LANES), lambda g, b, i: (g, 0, 0, 0)))
        args.append(sink)
    return pl.pallas_call(
        functools.partial(_band_kernel, n_seg=len(segs), seg_bias=tuple(seg_bias), n_slab=n_slab,
                          n_bias=n_bias, has_sink=sink is not None, tq=tq),
        grid=(G, B, nq), in_specs=in_specs,
        out_specs=pl.BlockSpec((1, tq, n_slab * LANES), lambda g, b, i: (b, i, g)),
        out_shape=jax.ShapeDtypeStruct(q.shape, BF16),
        compiler_params=_cparams(("parallel", "parallel", "parallel")), name=name,
    )(*args)


_OWN = lambda i, nq: i
_PREV = lambda i, nq: jnp.maximum(i - 1, 0)
_NEXT = lambda i, nq: jnp.minimum(i + 1, nq - 1)
_FIRST = lambda i, nq: 0


def _post_kernel(x_ref, o_ref, mod_ref, g_ref, b_ref, wo_ref, w1_ref, w2_ref, out_ref, *, cf):
    x = x_ref[0]
    y = _dot(o_ref[0], wo_ref[...])
    x1 = _layer_norm(ALPHA * x + mod_ref[0, 2:3, :] * y, g_ref[0:1, :], b_ref[0:1, :])
    h = (x1 * (1.0 + mod_ref[0, 4:5, :]) + mod_ref[0, 3:4, :]).astype(BF16)
    acc = jnp.zeros(x.shape, F32)
    for c in range(0, D_FF, cf):
        u = jnp.maximum(_dot(h, w1_ref[:, c:c + cf]), 0.0)
        acc = acc + _dot((u * u).astype(BF16), w2_ref[c:c + cf, :])
    out_ref[0] = _layer_norm(ALPHA * x1 + mod_ref[0, 5:6, :] * acc, g_ref[1:2, :], b_ref[1:2, :])


def _post(x, o, mod, row_of, ln_g, ln_b, wo, w1, w2, tm=256, name="post"):
    B, T, _ = x.shape
    tm = min(tm, T)
    const = lambda b, i: (0, 0)
    once = pl.Buffered(1)
    return pl.pallas_call(
        functools.partial(_post_kernel, cf=1024),
        grid=(B, T // tm),
        in_specs=[pl.BlockSpec((1, tm, D_MODEL), lambda b, i: (b, i, 0)),
                  pl.BlockSpec((1, tm, D_MODEL), lambda b, i: (b, i, 0)),
                  pl.BlockSpec((1, 6, D_MODEL), lambda b, i: (row_of(b), 0, 0)),
                  pl.BlockSpec((2, D_MODEL), const), pl.BlockSpec((2, D_MODEL), const),
                  pl.BlockSpec(wo.shape, const, pipeline_mode=once),
                  pl.BlockSpec(w1.shape, const, pipeline_mode=once),
                  pl.BlockSpec(w2.shape, const, pipeline_mode=once)],
        out_specs=pl.BlockSpec((1, tm, D_MODEL), lambda b, i: (b, i, 0)),
        out_shape=jax.ShapeDtypeStruct(x.shape, F32),
        compiler_params=_cparams(("parallel", "parallel")), name=name,
    )(x, o, mod, ln_g, ln_b, wo, w1, w2)


def _rope_tables(T, R, lead, reps):
    n = R // 4
    t = jnp.arange(T)
    inv = ROPE_BASE ** (-jnp.arange(n, dtype=jnp.float32) / n)

    def cs(p):
        ang = p.astype(jnp.float32)[:, None] * inv
        return jnp.cos(ang), jnp.sin(ang)

    cr, sr = cs(t // GRID_W)
    cc, sc = cs(t % GRID_W)
    z = jnp.zeros_like(sr)
    cos = jnp.concatenate([cr, cr, cc, cc], -1)
    sup = jnp.concatenate([-sr, z, -sc, z], -1)
    sdn = jnp.concatenate([z, sr, z, sc], -1)
    tail = LANES - lead - reps * R

    def slab(a, fill):
        parts = [jnp.full((T, lead), fill, F32)] if lead else []
        parts += [a] * reps
        if tail:
            parts.append(jnp.full((T, tail), fill, F32))
        return jnp.concatenate(parts, -1)

    return slab(cos, 1.0), slab(sup, 0.0), slab(sdn, 0.0)


def _na_bias_tables(rpb, rows):
    qr, kr_n = 4, 12
    tabs = []
    for r_base in (0, qr, rows - qr):
        r = r_base + np.arange(qr)[:, None, None, None]
        c = np.arange(GRID_W)[None, :, None, None]
        kr = r_base - qr + np.arange(kr_n)[None, None, :, None]
        kc = np.arange(GRID_W)[None, None, None, :]
        r0 = np.clip(r - NA_KH // 2, 0, rows - NA_KH)
        c0 = np.clip(c - NA_KW // 2, 0, GRID_W - NA_KW)
        valid = (kr >= r0) & (kr < r0 + NA_KH) & (kc >= c0) & (kc < c0 + NA_KW)
        row_rel = np.clip(kr - r + NA_KH - 1, 0, 2 * NA_KH - 2) + 0 * kc + 0 * c
        col_rel = np.clip(kc - c + NA_KW - 1, 0, 2 * NA_KW - 2) + 0 * kr + 0 * r
        shape = (qr * GRID_W, kr_n * GRID_W)
        valid = np.broadcast_to(valid, (qr, GRID_W, kr_n, GRID_W)).reshape(shape)
        vals = rpb.astype(F32)[:, row_rel.reshape(shape), col_rel.reshape(shape)]
        tabs.append(jnp.where(valid[None], vals, NEG_INF))
    return jnp.stack(tabs, 0)


def _swa_mask_tables(nb):
    qi = np.arange(SWA_QB)[:, None]
    kj = np.arange(3 * SWA_QB)[None, :]
    tabs = []
    for blk in (0, 1, nb - 1):
        kpos = (blk - 1) * SWA_QB + kj
        ok = (np.abs(kj - SWA_QB - qi) <= SWA_WINDOW) & (kpos >= 0) & (kpos < nb * SWA_QB)
        tabs.append(np.where(ok, 0.0, NEG_INF).astype(np.float32)[None])
    return jnp.asarray(np.stack(tabs, 0))


_SWA_PERM = np.array([8 * p + 4 * half + s for p in range(2) for s in range(4) for half in range(2)])


def _perm_heads(w, axis, dh=HALF):
    shape = w.shape
    w = w.reshape(shape[:axis] + (SWA_HEADS, dh) + shape[axis + 1:])
    w = jnp.take(w, jnp.asarray(_SWA_PERM), axis=axis)
    return w.reshape(shape)


def kernel(x_prompt, x_sample, cache_l0_k, cache_l0_v, cache_l1_k, cache_l1_v, cache_l2_ckv, cache_l2_kpe, cache_l3_k, cache_l3_v, c, c_ctx, w_mod, b_mod, ln_g, ln_b, w_mlp1, w_mlp2, l0_w_qkv, l0_lam, l0_subln, l0_w_o, l1_w_qkv, l1_rpb, l1_w_o, l2_w_a, l2_q_norm, l2_kv_norm, l2_w_uq, l2_w_ukv, l2_w_o, l3_w_qkv, l3_sink, l3_w_o):
    D = D_MODEL
    Bc, S, _ = x_prompt.shape
    Bl, T, _ = x_sample.shape
    P = cache_l0_k.shape[1]
    assert 1 + Bl <= MOD_ROWS and T % GRID_W == 0
    rows = T // GRID_W

    cond = jnp.concatenate([c_ctx[None], c, jnp.zeros((MOD_ROWS - 1 - Bl, D), F32)], 0)
    mod = _modulation(cond, w_mod, b_mod)
    ctx_row = lambda b: 0
    lat_row = lambda b: b + 1

    bf = lambda a: a.astype(BF16)
    tr = lambda a: jnp.swapaxes(a, -1, -2)
    qscale = lambda w, nq, s: jnp.concatenate([w[:, :nq] * s, w[:, nq:]], 1)
    w1 = bf(w_mlp1)
    w2 = bf(w_mlp2)

    tabs64 = _rope_tables(T, DIFF_DH, 0, 2)
    rope64 = (DIFF_DH // 4, tabs64)
    rope64_t = (DIFF_DH // 4, tuple(tr(t) for t in tabs64))

    w0 = qscale(l0_w_qkv, D, DIFF_DH ** -0.5)
    w0_k = bf(w0[:, D:])
    w0_t = bf(tr(jnp.concatenate([w0[:, :D], w0[:, 2 * D:]], 1)))
    wo0 = bf(l0_w_o)
    lam = l0_lam.astype(F32)
    subln = l0_subln.reshape(2 * DIFF_DH, 1)
    k, k0, v0, qt, vt = _project(x_prompt, mod[0], ctx_row, w0_k, w0_t,
                                 [(0, D, False, True, True), (D, D, False, False, True)],
                                 [(0, D, False, True, False), (D, D, False, True, False)], name="l0_proj_ctx")
    o = _flash(qt, k, vt, mode="diff", extra=(lam, subln), name="l0_attn_ctx")
    xp = _post(x_prompt, o, mod[0], ctx_row, ln_g[0], ln_b[0], wo0, w1[0], w2[0], name="l0_post_ctx")
    k, qt, vt = _project(x_sample, mod[0], lat_row, w0_k, w0_t, [(0, D, True, True, False)],
                         [(0, D, True, True, False), (D, D, False, True, False)],
                         rope=rope64, rope_t=rope64_t, name="l0_proj_lat")
    cache = (bf(cache_l0_k.reshape(Bl, P, D)), bf(tr(cache_l0_v.reshape(Bl, P, D))))
    o = _flash(qt, k, vt, cache, mode="diff", extra=(lam, subln), name="l0_attn_lat")
    xs = _post(x_sample, o, mod[0], lat_row, ln_g[0], ln_b[0], wo0, w1[0], w2[0], name="l0_post_lat")

    w = bf(qscale(l1_w_qkv, D, 64 ** -0.5))
    wo = bf(l1_w_o)
    q, k, k1, v, v1 = _project(xp, mod[1], ctx_row, w, None,
                               [(0, D, False, True, False), (D, D, False, True, True), (2 * D, D, False, True, True)],
                               [], name="l1_proj_ctx")
    o = _band(q, [(k, v, S, _OWN, None)], n_slab=1, tq=S, name="l1_attn_ctx")
    xp = _post(xp, o, mod[1], ctx_row, ln_g[1], ln_b[1], wo, w1[1], w2[1], name="l1_post_ctx")
    q, k, v = _project(xs, mod[1], lat_row, w, None,
                       [(0, D, False, True, False), (D, D, False, True, False), (2 * D, D, False, True, False)],
                       [], name="l1_proj_lat")
    kc, vc = bf(cache_l1_k.reshape(Bl, P, D)), bf(cache_l1_v.reshape(Bl, P, D))
    tqn = 4 * GRID_W
    bias = _na_bias_tables(l1_rpb, rows)
    o = _band(q, [(k, v, tqn, _PREV, 0), (k, v, tqn, _OWN, tqn), (k, v, tqn, _NEXT, 2 * tqn),
                  (kc, vc, P, _FIRST, None)], n_slab=1, tq=tqn, bias=bias, name="l1_attn_lat")
    xs = _post(xs, o, mod[1], lat_row, ln_g[1], ln_b[1], wo, w1[1], w2[1], name="l1_post_lat")

    H, dq = MLA_HEADS, MLA_NOPE + MLA_ROPE
    wa = bf(jnp.pad(l2_w_a, ((0, 0), (0, 7 * LANES - l2_w_a.shape[1]))))
    wuq = jnp.pad(l2_w_uq.reshape(MLA_Q_RANK, H, dq), ((0, 0), (0, 0), (0, LANES - dq))).reshape(MLA_Q_RANK, H * LANES)
    wuqt = bf(tr(wuq))
    wukv = l2_w_ukv.reshape(MLA_KV_RANK, H, MLA_NOPE + MLA_VDIM)
    wk_n = jnp.pad(wukv[:, :, :MLA_NOPE], ((0, 0), (0, 0), (0, LANES - MLA_NOPE))).reshape(MLA_KV_RANK, H * LANES)
    place = np.zeros((LANES, H, LANES), np.float32)
    place[np.arange(MLA_ROPE), :, MLA_NOPE + np.arange(MLA_ROPE)] = 1.0
    wk = bf(jnp.concatenate([wk_n, jnp.asarray(place.reshape(LANES, H * LANES))], 0))
    wvt = bf(tr(wukv[:, :, MLA_NOPE:].reshape(MLA_KV_RANK, H * MLA_VDIM)))
    wo = bf(l2_w_o)
    qn, kvn = l2_q_norm.reshape(1, -1), l2_kv_norm.reshape(1, -1)
    mscale = dq ** -0.5
    qt, k, vt, ckv2, kpe2 = _mla_project(xp, mod[2], ctx_row, wa, qn, kvn, wuqt, wk, wvt, mscale, name="l2_proj_ctx")
    o = _flash(qt, k, vt, mode="mla", name="l2_attn_ctx")
    xp = _post(xp, o, mod[2], ctx_row, ln_g[2], ln_b[2], wo, w1[2], w2[2], name="l2_post_ctx")
    tabs = tuple(tr(t) for t in _rope_tables(T, MLA_ROPE, MLA_NOPE, 1)) + _rope_tables(T, MLA_ROPE, 0, 1)
    qt, k, vt = _mla_project(xs, mod[2], lat_row, wa, qn, kvn, wuqt, wk, wvt, mscale, rope=tabs, name="l2_proj_lat")
    kpe_slab = jnp.pad(cache_l2_kpe, ((0, 0), (0, 0), (0, LANES - MLA_ROPE)))
    cache = _mla_cache_expand(cache_l2_ckv, kpe_slab, wk, wvt)
    o = _flash(qt, k, vt, cache, mode="mla", name="l2_attn_lat")
    xs = _post(xs, o, mod[2], lat_row, ln_g[2], ln_b[2], wo, w1[2], w2[2], name="l2_post_lat")

    nq3, nk3 = SWA_HEADS * HALF, SWA_KV_HEADS * HALF
    w = jnp.concatenate([_perm_heads(l3_w_qkv[:, :nq3], 1) * (HALF ** -0.5), l3_w_qkv[:, nq3:]], 1)
    w = bf(w)
    wo = bf(_perm_heads(l3_w_o, 0))
    sink = jnp.take(l3_sink.astype(F32), jnp.asarray(_SWA_PERM))
    sink = jnp.repeat(sink.reshape(2, 4, 1, 2), HALF, axis=-1)
    q, k, k3, v, v3 = _project(xp, mod[3], ctx_row, w, None,
                               [(0, nq3, False, True, False), (nq3, nk3, False, True, True),
                                (nq3 + nk3, nk3, False, True, True)], [], name="l3_proj_ctx")
    o = _band(q, [(k, v, S, _OWN, None)], n_slab=4, tq=S, sink=sink, name="l3_attn_ctx")
    xp = _post(xp, o, mod[3], ctx_row, ln_g[3], ln_b[3], wo, w1[3], w2[3], name="l3_post_ctx")
    q, k, v = _project(xs, mod[3], lat_row, w, None,
                       [(0, nq3, True, True, False), (nq3, nk3, True, True, False),
                        (nq3 + nk3, nk3, False, True, False)], [], rope=rope64, name="l3_proj_lat")
    kc, vc = bf(cache_l3_k.reshape(Bl, P, nk3)), bf(cache_l3_v.reshape(Bl, P, nk3))
    mask = _swa_mask_tables(T // SWA_QB)
    o = _band(q, [(k, v, SWA_QB, _PREV, 0), (k, v, SWA_QB, _OWN, SWA_QB), (k, v, SWA_QB, _NEXT, 2 * SWA_QB),
                  (kc, vc, P, _FIRST, None)], n_slab=4, tq=SWA_QB, bias=mask, sink=sink, name="l3_attn_lat")
    xs = _post(xs, o, mod[3], lat_row, ln_g[3], ln_b[3], wo, w1[3], w2[3], name="l3_post_lat")

    return (xp, xs,
            k0.reshape(Bc, S, DIFF_HEADS, 2 * DIFF_DH), v0.reshape(Bc, S, DIFF_HEADS, 2 * DIFF_DH),
            k1.reshape(Bc, S, NA_HEADS, HALF), v1.reshape(Bc, S, NA_HEADS, HALF),
            ckv2, kpe2,
            k3.reshape(Bc, S, SWA_KV_HEADS, HALF), v3.reshape(Bc, S, SWA_KV_HEADS, HALF))
```

```python
import functools
import math

import numpy as np
import jax
import jax.numpy as jnp
from jax import lax
from jax.experimental import pallas as pl
from jax.experimental.pallas import tpu as pltpu

D_MODEL = 1024
DEPTH = 4
GRID_W = 64
ROPE_BASE = 10000.0
LN_EPS = 1e-5
NEG_INF = -1e30
ALPHA = (2 * DEPTH) ** 0.25
D_FF = 4 * D_MODEL
DIFF_DH = 64
DIFF_HEADS = 8
DIFF_LAMBDA_INIT = 0.8 - 0.6 * math.exp(-0.3 * 0)
NA_HEADS = 16
NA_KH = 8
NA_KW = 16
NA_QROWS = 4
MLA_HEADS = 16
MLA_NOPE = 64
MLA_ROPE = 32
MLA_VDIM = 64
MLA_Q_RANK = 512
MLA_KV_RANK = 256
SWA_HEADS = 16
SWA_KV_HEADS = 4
SWA_WINDOW = 128
SWA_QB = 128
LOG2E = math.log2(math.e)

LANES = 128
HALF = LANES // 2
BF16_ROWS = 16
MOD_ROWS = 16
VMEM_LIMIT = 56 * 1024 * 1024

F32 = jnp.float32
BF16 = jnp.bfloat16


def _cparams(sem):
    return pltpu.CompilerParams(dimension_semantics=sem, vmem_limit_bytes=VMEM_LIMIT)


def _dot(a, b):
    return jnp.dot(a, b, preferred_element_type=F32)


def _layer_norm(x, g, b):
    mu = jnp.mean(x, -1, keepdims=True)
    xc = x - mu
    var = jnp.mean(xc * xc, -1, keepdims=True)
    return xc * lax.rsqrt(var + LN_EPS) * g + b


def _rms_norm(x, g, axis=-1):
    return x * lax.rsqrt(jnp.mean(x * x, axis, keepdims=True) + LN_EPS) * g


def _rope_slab(x, cos, sin_up, sin_dn, n):
    return x * cos + pltpu.roll(x, LANES - n, 1) * sin_up + pltpu.roll(x, n, 1) * sin_dn


def _rope_slab_t(x, cos, sin_up, sin_dn, n):
    up = jnp.concatenate([x[n:], x[:n]], axis=0)
    dn = jnp.concatenate([x[LANES - n:], x[:LANES - n]], axis=0)
    return x * cos + up * sin_up + dn * sin_dn


def _split_heads_t(qt):
    lo = lax.broadcasted_iota(jnp.int32, qt.shape, 0) < HALF
    zero = jnp.zeros_like(qt)
    return jnp.concatenate([jnp.where(lo, qt, zero), jnp.where(lo, zero, qt)], axis=1)


def _with_ones_row(vt):
    r = lax.broadcasted_iota(jnp.int32, (BF16_ROWS, vt.shape[1]), 0)
    return jnp.concatenate([vt, jnp.where(r == 0, 1.0, 0.0).astype(BF16)], axis=0)


def _modulated(x_ref, mod_ref):
    return x_ref[0] * (1.0 + mod_ref[0, 1:2, :]) + mod_ref[0, 0:1, :]


def _mod_kernel(c_ref, w_ref, b_ref, o_ref):
    c = c_ref[...]
    s = c / (1.0 + jnp.exp(-c))
    o_ref[0] = _dot(s.astype(BF16), w_ref[0].astype(BF16)) + b_ref[0]


def _modulation(cond, w_mod, b_mod):
    tn = 512
    n = 6 * D_MODEL
    out = pl.pallas_call(
        _mod_kernel,
        grid=(DEPTH, n // tn),
        in_specs=[pl.BlockSpec((MOD_ROWS, D_MODEL), lambda l, j: (0, 0)),
                  pl.BlockSpec((1, D_MODEL, tn), lambda l, j: (l, 0, j)),
                  pl.BlockSpec((1, 1, tn), lambda l, j: (l, 0, j))],
        out_specs=pl.BlockSpec((1, MOD_ROWS, tn), lambda l, j: (l, 0, j)),
        out_shape=jax.ShapeDtypeStruct((DEPTH, MOD_ROWS, n), F32),
        compiler_params=_cparams(("parallel", "parallel")),
        name="modulation",
    )(cond, w_mod, b_mod.reshape(DEPTH, 1, n))
    return out.reshape(DEPTH, MOD_ROWS, 6, D_MODEL)


def _proj_kernel(*refs, segs, tsegs, rope_n, cw):
    x_ref, mod_ref = refs[:2]
    pos = 2
    w_ref = wt_ref = None
    if segs:
        w_ref = refs[pos]
        pos += 1
    if tsegs:
        wt_ref = refs[pos]
        pos += 1
    if rope_n and any(s[2] for s in segs):
        cos, sup, sdn = (r[...] for r in refs[pos:pos + 3])
        pos += 3
    if rope_n and any(s[2] for s in tsegs):
        cos_t, sup_t, sdn_t = (r[...] for r in refs[pos:pos + 3])
        pos += 3
    outs = list(refs[pos:])
    h32 = _modulated(x_ref, mod_ref)
    h = h32.astype(BF16)
    for (start, width, rope, want16, want32, mul) in segs:
        o16 = outs.pop(0) if want16 else None
        o32 = outs.pop(0) if want32 else None
        for c in range(0, width, cw):
            acc = _dot(h, w_ref[:, start + c:start + c + cw])
            if want32:
                o32[0, :, c:c + cw] = acc
            if mul is not None:
                acc = acc * mul
            if rope:
                acc = jnp.concatenate(
                    [_rope_slab(acc[:, s:s + LANES], cos, sup, sdn, rope_n) for s in range(0, cw, LANES)], axis=1)
            if want16:
                o16[0, :, c:c + cw] = acc.astype(BF16)
    if tsegs:
        ht = h32.T.astype(BF16)
        for (start, width, rope, _, _, mul) in tsegs:
            o16 = outs.pop(0)
            for c in range(0, width, cw):
                acc = _dot(wt_ref[start + c:start + c + cw, :], ht)
                if mul is not None:
                    acc = acc * mul
                if rope:
                    acc = jnp.concatenate(
                        [_rope_slab_t(acc[s:s + LANES], cos_t, sup_t, sdn_t, rope_n) for s in range(0, cw, LANES)],
                        axis=0)
                o16[0, c:c + cw, :] = acc.astype(BF16)


def _project(x, mod, row_of, w, wt, segs, tsegs, rope=None, rope_t=None, tm=512, cw=256, name="proj"):
    B, T, _ = x.shape
    tm = min(tm, T)
    const = lambda b, i: (0, 0)
    in_specs = [pl.BlockSpec((1, tm, D_MODEL), lambda b, i: (b, i, 0)),
                pl.BlockSpec((1, 6, D_MODEL), lambda b, i: (row_of(b), 0, 0))]
    args = [x, mod]
    if segs:
        in_specs.append(pl.BlockSpec(w.shape, const))
        args.append(w)
    if tsegs:
        in_specs.append(pl.BlockSpec(wt.shape, const))
        args.append(wt)
    rope_n = 0
    if rope is not None:
        rope_n, tabs = rope
        in_specs += [pl.BlockSpec((tm, LANES), lambda b, i: (i, 0))] * 3
        args += list(tabs)
    if rope_t is not None:
        rope_n, tabs = rope_t
        in_specs += [pl.BlockSpec((LANES, tm), lambda b, i: (0, i))] * 3
        args += list(tabs)
    out_specs, out_shape = [], []
    for (_, width, _, want16, want32, _) in segs:
        for want, dt in ((want16, BF16), (want32, F32)):
            if want:
                out_specs.append(pl.BlockSpec((1, tm, width), lambda b, i: (b, i, 0)))
                out_shape.append(jax.ShapeDtypeStruct((B, T, width), dt))
    for (_, width, _, _, _, _) in tsegs:
        out_specs.append(pl.BlockSpec((1, width, tm), lambda b, i: (b, 0, i)))
        out_shape.append(jax.ShapeDtypeStruct((B, width, T), BF16))
    return pl.pallas_call(
        functools.partial(_proj_kernel, segs=tuple(segs), tsegs=tuple(tsegs), rope_n=rope_n, cw=cw),
        grid=(B, T // tm), in_specs=in_specs, out_specs=out_specs, out_shape=out_shape,
        compiler_params=_cparams(("parallel", "parallel")), name=name,
    )(*args)


def _mla_proj_kernel(*refs, lat, scale, cw):
    x_ref, mod_ref, wa_ref, qn_ref, kvn_ref, wuqt_ref, wk_ref, wvt_ref = refs[:8]
    pos = 8
    if lat:
        qcos, qsup, qsdn, kcos, ksup, ksdn = (r[...] for r in refs[8:14])
        pos = 14
    qt_out, k_out, vt_out = refs[pos:pos + 3]
    h = _modulated(x_ref, mod_ref).astype(BF16)
    a = _dot(h, wa_ref[...])
    cq = _rms_norm(a[:, :MLA_Q_RANK], qn_ref[...])
    ckv = _rms_norm(a[:, MLA_Q_RANK:MLA_Q_RANK + MLA_KV_RANK], kvn_ref[...])
    kpe = a[:, MLA_Q_RANK + MLA_KV_RANK:]
    if lat:
        kpe = _rope_slab(kpe, kcos, ksup, ksdn, MLA_ROPE // 4)
    else:
        ckv_state, kpe_state = refs[pos + 3:pos + 5]
        ckv_state[0] = ckv
        kpe_state[0] = kpe[:, :MLA_ROPE]
    cqt = cq.T.astype(BF16)
    n_q = wuqt_ref.shape[0]
    for c in range(0, n_q, cw):
        q = _dot(wuqt_ref[c:c + cw, :], cqt) * scale
        if lat:
            q = jnp.concatenate(
                [_rope_slab_t(q[s:s + LANES], qcos, qsup, qsdn, MLA_ROPE // 4) for s in range(0, cw, LANES)], axis=0)
        qt_out[0, c:c + cw, :] = q.astype(BF16)
    kin = jnp.concatenate([ckv.astype(BF16), kpe.astype(BF16)], axis=1)
    for c in range(0, wk_ref.shape[1], cw):
        k_out[0, :, c:c + cw] = _dot(kin, wk_ref[:, c:c + cw]).astype(BF16)
    ckvt = ckv.T.astype(BF16)
    for c in range(0, wvt_ref.shape[0], cw):
        vt_out[0, c:c + cw, :] = _dot(wvt_ref[c:c + cw, :], ckvt).astype(BF16)


def _mla_project(x, mod, row_of, wa, qn, kvn, wuqt, wk, wvt, scale, rope=None, tm=512, name="mla_proj"):
    B, T, _ = x.shape
    tm = min(tm, T)
    lat = rope is not None
    const = lambda b, i: (0, 0)
    in_specs = [pl.BlockSpec((1, tm, D_MODEL), lambda b, i: (b, i, 0)),
                pl.BlockSpec((1, 6, D_MODEL), lambda b, i: (row_of(b), 0, 0))]
    in_specs += [pl.BlockSpec(a.shape, const) for a in (wa, qn, kvn, wuqt, wk, wvt)]
    args = [x, mod, wa, qn, kvn, wuqt, wk, wvt]
    if lat:
        in_specs += [pl.BlockSpec((LANES, tm), lambda b, i: (0, i))] * 3
        in_specs += [pl.BlockSpec((tm, LANES), lambda b, i: (i, 0))] * 3
        args += list(rope)
    tok = lambda b, i: (b, i, 0)
    feat = lambda b, i: (b, 0, i)
    out_specs = [pl.BlockSpec((1, wuqt.shape[0], tm), feat), pl.BlockSpec((1, tm, wk.shape[1]), tok),
                 pl.BlockSpec((1, wvt.shape[0], tm), feat)]
    out_shape = [jax.ShapeDtypeStruct((B, wuqt.shape[0], T), BF16), jax.ShapeDtypeStruct((B, T, wk.shape[1]), BF16),
                 jax.ShapeDtypeStruct((B, wvt.shape[0], T), BF16)]
    if not lat:
        out_specs += [pl.BlockSpec((1, tm, MLA_KV_RANK), tok), pl.BlockSpec((1, tm, MLA_ROPE), tok)]
        out_shape += [jax.ShapeDtypeStruct((B, T, MLA_KV_RANK), F32), jax.ShapeDtypeStruct((B, T, MLA_ROPE), F32)]
    return pl.pallas_call(
        functools.partial(_mla_proj_kernel, lat=lat, scale=scale, cw=512),
        grid=(B, T // tm), in_specs=in_specs, out_specs=out_specs, out_shape=out_shape,
        compiler_params=_cparams(("parallel", "parallel")), name=name,
    )(*args)


def _mla_cache_kernel(ckv_ref, kpe_ref, wk_ref, wvt_ref, k_out, vt_out):
    ckv = ckv_ref[0]
    kin = jnp.concatenate([ckv.astype(BF16), kpe_ref[0].astype(BF16)], axis=1)
    k_out[0] = _dot(kin, wk_ref[...]).astype(BF16)
    vt_out[0] = _dot(wvt_ref[...], ckv.T.astype(BF16)).astype(BF16)


def _mla_cache_expand(c_ckv, c_kpe_slab, wk, wvt):
    B, P, _ = c_ckv.shape
    const = lambda b: (0, 0)
    return pl.pallas_call(
        _mla_cache_kernel, grid=(B,),
        in_specs=[pl.BlockSpec((1, P, MLA_KV_RANK), lambda b: (b, 0, 0)),
                  pl.BlockSpec((1, P, LANES), lambda b: (b, 0, 0)),
                  pl.BlockSpec(wk.shape, const), pl.BlockSpec(wvt.shape, const)],
        out_specs=[pl.BlockSpec((1, P, wk.shape[1]), lambda b: (b, 0, 0)),
                   pl.BlockSpec((1, wvt.shape[0], P), lambda b: (b, 0, 0))],
        out_shape=[jax.ShapeDtypeStruct((B, P, wk.shape[1]), BF16),
                   jax.ShapeDtypeStruct((B, wvt.shape[0], P), BF16)],
        compiler_params=_cparams(("parallel",)), name="mla_cache_expand",
    )(c_ckv, c_kpe_slab, wk, wvt)


def _flash_kernel(*refs, mode, has_cache, tq, tk):
    qt_ref = refs[0]
    pos = 1
    if has_cache:
        kc_ref, vct_ref = refs[1:3]
        pos = 3
    k_ref, vt_ref = refs[pos:pos + 2]
    pos += 2
    if mode == "diff":
        lam_ref, subln_ref = refs[pos:pos + 2]
        pos += 2
    o_ref, s_sc, acc_sc = refs[pos:pos + 3]
    dv = LANES if mode == "diff" else HALF

    qt = qt_ref[0]
    if mode == "diff":
        qst = _split_heads_t(qt)

    blocks = []
    if has_cache:
        blocks += [(kc_ref, vct_ref, j) for j in range(kc_ref.shape[1] // tk)]
    blocks += [(k_ref, vt_ref, j) for j in range(k_ref.shape[1] // tk)]

    def logits(n):
        kr, _, j = blocks[n]
        k = kr[0, j * tk:(j + 1) * tk, :]
        if mode == "diff":
            s_sc[n % 2] = _dot(k, qst)
        else:
            s_sc[n % 2, :, :tq] = _dot(k[:, :LANES], qt[:LANES])
            s_sc[n % 2, :, tq:] = _dot(k[:, LANES:], qt[LANES:])

    logits(0)
    m = None
    for n in range(len(blocks)):
        if n + 1 < len(blocks):
            logits(n + 1)
        _, vr, j = blocks[n]
        vt = vr[0, :, j * tk:(j + 1) * tk]
        s = s_sc[n % 2]
        m_cur = jnp.max(s, 0, keepdims=True)
        m_new = m_cur if m is None else jnp.maximum(m, m_cur)
        pb = jnp.exp2(s - m_new).astype(BF16)
        if mode == "diff":
            pv = _dot(_with_ones_row(vt), pb)
        else:
            pv = jnp.concatenate([_dot(_with_ones_row(vt[:HALF]), pb[:, :tq]),
                                  _dot(_with_ones_row(vt[HALF:]), pb[:, tq:])], axis=1)
        if m is None:
            acc_sc[...] = pv
        else:
            acc_sc[...] = jnp.exp2(m - m_new) * acc_sc[...] + pv
        m = m_new

    acc = acc_sc[...]
    o = acc[:dv] / acc[dv:dv + 1]
    o0, o1 = o[:, :tq], o[:, tq:]
    if mode == "diff":
        lam = lam_ref[...]
        lam_full = (jnp.exp(jnp.sum(lam[0:1] * lam[1:2], keepdims=True))
                    - jnp.exp(jnp.sum(lam[2:3] * lam[3:4], keepdims=True)) + DIFF_LAMBDA_INIT)
        out_t = _rms_norm(o0 - lam_full * o1, subln_ref[...], axis=0) * (1.0 - DIFF_LAMBDA_INIT)
    else:
        out_t = jnp.concatenate([o0, o1], axis=0)
    o_ref[0] = out_t.T.astype(o_ref.dtype)


def _flash(qt, k, vt, cache=None, *, mode, extra=(), tq=256, tk=512, name="flash"):
    B, _, T = qt.shape
    S = k.shape[1]
    wq = LANES if mode == "diff" else 2 * LANES
    G = vt.shape[1] // LANES
    tq, tk = min(tq, T), min(tk, S)
    in_specs = [pl.BlockSpec((1, wq, tq), lambda b, g, i: (b, g, i))]
    args = [qt]
    if cache is not None:
        kc, vct = cache
        P = kc.shape[1]
        in_specs += [pl.BlockSpec((1, P, wq), lambda b, g, i: (b, 0, g)),
                     pl.BlockSpec((1, LANES, P), lambda b, g, i: (b, g, 0))]
        args += [kc, vct]
    in_specs += [pl.BlockSpec((1, S, wq), lambda b, g, i: (b, 0, g)),
                 pl.BlockSpec((1, LANES, S), lambda b, g, i: (b, g, 0))]
    args += [k, vt]
    for e in extra:
        in_specs.append(pl.BlockSpec(e.shape, lambda b, g, i: (0, 0)))
        args.append(e)
    dv = LANES if mode == "diff" else HALF
    return pl.pallas_call(
        functools.partial(_flash_kernel, mode=mode, has_cache=cache is not None, tq=tq, tk=tk),
        grid=(B, G, T // tq), in_specs=in_specs,
        out_specs=pl.BlockSpec((1, tq, LANES), lambda b, g, i: (b, i, g)),
        out_shape=jax.ShapeDtypeStruct((B, T, G * LANES), BF16),
        scratch_shapes=[pltpu.VMEM((2, tk, 2 * tq), F32), pltpu.VMEM((dv + BF16_ROWS, 2 * tq), F32)],
        compiler_params=_cparams(("parallel", "parallel", "parallel")), name=name,
    )(*args)


def _band_kernel(*refs, n_seg, seg_bias, n_slab, has_bias, has_sink, tq):
    qt_ref = refs[0]
    kv = refs[1:1 + 2 * n_seg]
    pos = 1 + 2 * n_seg
    bias_ref = sink_ref = None
    if has_bias:
        bias_ref = refs[pos]
        pos += 1
    if has_sink:
        sink_ref = refs[pos]
        pos += 1
    o_ref = refs[pos]
    qst = jnp.concatenate([_split_heads_t(qt_ref[0, s * LANES:(s + 1) * LANES, :]) for s in range(n_slab)], axis=1)
    logits = []
    for i in range(n_seg):
        lg = _dot(kv[2 * i][0], qst)
        if seg_bias[i] is not None:
            lg = lg + bias_ref[0, 0, seg_bias[i]:seg_bias[i] + lg.shape[0], :]
        logits.append(lg)
    m = functools.reduce(jnp.maximum, [jnp.max(lg, 0, keepdims=True) for lg in logits])
    if has_sink:
        sink = sink_ref[0]
        m = jnp.maximum(m, sink)
    acc = None
    for i in range(n_seg):
        pv = _dot(_with_ones_row(kv[2 * i + 1][0]), jnp.exp2(logits[i] - m).astype(BF16))
        acc = pv if acc is None else acc + pv
    l = acc[LANES:LANES + 1]
    if has_sink:
        l = l + jnp.exp2(sink - m)
    o = acc[:LANES] / l
    for s in range(n_slab):
        c = 2 * s * tq
        out_t = jnp.concatenate([o[:HALF, c:c + tq], o[HALF:, c + tq:c + 2 * tq]], axis=0)
        o_ref[0, :, s * LANES:(s + 1) * LANES] = out_t.T.astype(o_ref.dtype)


def _band(qt, segs, *, n_slab, tq, bias=None, sink=None, name="band"):
    B, _, T = qt.shape
    G = qt.shape[1] // (n_slab * LANES)
    nq = T // tq
    R = n_slab * 2 * tq
    in_specs = [pl.BlockSpec((1, n_slab * LANES, tq), lambda g, b, i: (b, g, i))]
    args = [qt]
    seg_bias = []
    for (k, vt, ts, blk_of, boff) in segs:
        in_specs += [pl.BlockSpec((1, ts, LANES), (lambda f: lambda g, b, i: (b, f(i, nq), g))(blk_of)),
                     pl.BlockSpec((1, LANES, ts), (lambda f: lambda g, b, i: (b, g, f(i, nq)))(blk_of))]
        args += [k, vt]
        seg_bias.append(boff)
    if bias is not None:
        n_var, groups, sb, _ = bias.shape
        var_of = lambda i: jnp.where(i == 0, 0, jnp.where(i == nq - 1, n_var - 1, 1))
        in_specs.append(pl.BlockSpec((1, 1, sb, R), lambda g, b, i: (var_of(i), g if groups > 1 else 0, 0, 0)))
        args.append(bias)
    if sink is not None:
        in_specs.append(pl.BlockSpec((1, 1, R), lambda g, b, i: (g, 0, 0)))
        args.append(sink)
    return pl.pallas_call(
        functools.partial(_band_kernel, n_seg=len(segs), seg_bias=tuple(seg_bias), n_slab=n_slab,
                          has_bias=bias is not None, has_sink=sink is not None, tq=tq),
        grid=(G, B, nq), in_specs=in_specs,
        out_specs=pl.BlockSpec((1, tq, n_slab * LANES), lambda g, b, i: (b, i, g)),
        out_shape=jax.ShapeDtypeStruct((B, T, G * n_slab * LANES), BF16),
        compiler_params=_cparams(("parallel", "parallel", "parallel")), name=name,
    )(*args)


_OWN = lambda i, nq: i
_PREV = lambda i, nq: jnp.maximum(i - 1, 0)
_NEXT = lambda i, nq: jnp.minimum(i + 1, nq - 1)
_FIRST = lambda i, nq: 0


def _post_kernel(x_ref, o_ref, mod_ref, g_ref, b_ref, wo_ref, w1_ref, w2_ref, out_ref, *, cf):
    x = x_ref[0]
    y = _dot(o_ref[0], wo_ref[...])
    x1 = _layer_norm(ALPHA * x + mod_ref[0, 2:3, :] * y, g_ref[0:1, :], b_ref[0:1, :])
    h = (x1 * (1.0 + mod_ref[0, 4:5, :]) + mod_ref[0, 3:4, :]).astype(BF16)
    acc = jnp.zeros(x.shape, F32)
    for c in range(0, D_FF, cf):
        u = jnp.maximum(_dot(h, w1_ref[:, c:c + cf]), 0.0)
        acc = acc + _dot((u * u).astype(BF16), w2_ref[c:c + cf, :])
    out_ref[0] = _layer_norm(ALPHA * x1 + mod_ref[0, 5:6, :] * acc, g_ref[1:2, :], b_ref[1:2, :])


def _post(x, o, mod, row_of, ln_g, ln_b, wo, w1, w2, tm=256, name="post"):
    B, T, _ = x.shape
    tm = min(tm, T)
    const = lambda b, i: (0, 0)
    once = pl.Buffered(1)
    return pl.pallas_call(
        functools.partial(_post_kernel, cf=1024),
        grid=(B, T // tm),
        in_specs=[pl.BlockSpec((1, tm, D_MODEL), lambda b, i: (b, i, 0)),
                  pl.BlockSpec((1, tm, D_MODEL), lambda b, i: (b, i, 0)),
                  pl.BlockSpec((1, 6, D_MODEL), lambda b, i: (row_of(b), 0, 0)),
                  pl.BlockSpec((2, D_MODEL), const), pl.BlockSpec((2, D_MODEL), const),
                  pl.BlockSpec(wo.shape, const, pipeline_mode=once),
                  pl.BlockSpec(w1.shape, const, pipeline_mode=once),
                  pl.BlockSpec(w2.shape, const, pipeline_mode=once)],
        out_specs=pl.BlockSpec((1, tm, D_MODEL), lambda b, i: (b, i, 0)),
        out_shape=jax.ShapeDtypeStruct(x.shape, F32),
        compiler_params=_cparams(("parallel", "parallel")), name=name,
    )(x, o, mod, ln_g, ln_b, wo, w1, w2)


def _rope_tables(T, R, lead, reps):
    n = R // 4
    t = jnp.arange(T)
    inv = ROPE_BASE ** (-jnp.arange(n, dtype=jnp.float32) / n)

    def cs(p):
        ang = p.astype(jnp.float32)[:, None] * inv
        return jnp.cos(ang), jnp.sin(ang)

    cr, sr = cs(t // GRID_W)
    cc, sc = cs(t % GRID_W)
    z = jnp.zeros_like(sr)
    cos = jnp.concatenate([cr, cr, cc, cc], -1)
    sup = jnp.concatenate([-sr, z, -sc, z], -1)
    sdn = jnp.concatenate([z, sr, z, sc], -1)
    tail = LANES - lead - reps * R

    def slab(a, fill):
        parts = [jnp.full((T, lead), fill, F32)] if lead else []
        parts += [a] * reps
        if tail:
            parts.append(jnp.full((T, tail), fill, F32))
        return jnp.concatenate(parts, -1)

    return slab(cos, 1.0), slab(sup, 0.0), slab(sdn, 0.0)


def _na_bias_tables(rpb, rows):
    qr, kb_n = NA_QROWS, 3 * NA_QROWS
    n_dr, n_dc = 2 * NA_KH - 1, 2 * NA_KW - 1
    c = np.arange(GRID_W)
    c0 = np.clip(c - NA_KW // 2, 0, GRID_W - NA_KW)
    kc = np.arange(GRID_W)[:, None]
    col_ok = (kc >= c0[None]) & (kc < c0[None] + NA_KW)
    sel_c = ((np.arange(n_dc)[:, None, None] == (kc - c[None] + NA_KW - 1)[None]) & col_ok[None])
    sel_r = np.zeros((3, n_dr, kb_n, qr), bool)
    for v, r_base in enumerate((0, qr, rows - qr)):
        r = r_base + np.arange(qr)[None]
        kr = r_base - qr + np.arange(kb_n)[:, None]
        r0 = np.clip(r - NA_KH // 2, 0, rows - NA_KH)
        row_ok = (kr >= r0) & (kr < r0 + NA_KH)
        sel_r[v] = (np.arange(n_dr)[:, None, None] == (kr - r + NA_KH - 1)[None]) & row_ok[None]
    hp = lax.Precision.HIGHEST
    t1 = jnp.einsum("hde,ejc->hdjc", rpb.astype(F32), jnp.asarray(sel_c, F32), precision=hp)
    tab = jnp.einsum("vdkr,hdjc->vhkjrc", jnp.asarray(sel_r, F32), t1, precision=hp)
    ok = np.einsum("vdkr,ejc->vkjrc", sel_r.astype(np.int32), sel_c.astype(np.int32)) > 0
    tab = jnp.where(jnp.asarray(ok)[:, None], tab * LOG2E, NEG_INF)
    H = rpb.shape[0]
    tab = tab.reshape(3, H // 2, 2, kb_n * GRID_W, qr * GRID_W)
    return jnp.swapaxes(tab, 2, 3).reshape(3, H // 2, kb_n * GRID_W, 2 * qr * GRID_W)


def _swa_mask_tables(nb, reps):
    qi = np.arange(SWA_QB)[None, :]
    kj = np.arange(3 * SWA_QB)[:, None]
    tabs = []
    for blk in (0, 1, nb - 1):
        kpos = (blk - 1) * SWA_QB + kj
        ok = (np.abs(kj - SWA_QB - qi) <= SWA_WINDOW) & (kpos >= 0) & (kpos < nb * SWA_QB)
        tabs.append(np.tile(np.where(ok, 0.0, NEG_INF).astype(np.float32), (1, reps))[None])
    return jnp.asarray(np.stack(tabs, 0))


_SWA_PERM = np.array([8 * p + 4 * half + s for p in range(2) for s in range(4) for half in range(2)])


def _perm_heads(w, axis, dh=HALF):
    shape = w.shape
    w = w.reshape(shape[:axis] + (SWA_HEADS, dh) + shape[axis + 1:])
    w = jnp.take(w, jnp.asarray(_SWA_PERM), axis=axis)
    return w.reshape(shape)


def kernel(x_prompt, x_sample, cache_l0_k, cache_l0_v, cache_l1_k, cache_l1_v, cache_l2_ckv, cache_l2_kpe, cache_l3_k, cache_l3_v, c, c_ctx, w_mod, b_mod, ln_g, ln_b, w_mlp1, w_mlp2, l0_w_qkv, l0_lam, l0_subln, l0_w_o, l1_w_qkv, l1_rpb, l1_w_o, l2_w_a, l2_q_norm, l2_kv_norm, l2_w_uq, l2_w_ukv, l2_w_o, l3_w_qkv, l3_sink, l3_w_o):
    D = D_MODEL
    Bc, S, _ = x_prompt.shape
    Bl, T, _ = x_sample.shape
    P = cache_l0_k.shape[1]
    assert 1 + Bl <= MOD_ROWS and T % GRID_W == 0
    rows = T // GRID_W

    cond = jnp.concatenate([c_ctx[None], c, jnp.zeros((MOD_ROWS - 1 - Bl, D), F32)], 0)
    mod = _modulation(cond, w_mod, b_mod)
    ctx_row = lambda b: 0
    lat_row = lambda b: b + 1

    bf = lambda a: a.astype(BF16)
    tr = lambda a: jnp.swapaxes(a, -1, -2)
    w1 = bf(w_mlp1)
    w2 = bf(w_mlp2)
    qmul = HALF ** -0.5 * LOG2E

    tabs64 = _rope_tables(T, DIFF_DH, 0, 2)
    rope64 = (DIFF_DH // 4, tabs64)
    rope64_t = (DIFF_DH // 4, tuple(tr(t) for t in tabs64))

    def qkv_t(x, mod_l, row_of, w_k, w_t, nq, nk, lat, rope, name):
        if lat:
            return _project(x, mod_l, row_of, w_k, w_t, [(0, nk, rope, True, False, None)],
                            [(0, nq, rope, True, False, qmul), (nq, nk, False, True, False, None)],
                            rope=rope64 if rope else None, rope_t=rope64_t if rope else None, name=name)
        return _project(x, mod_l, row_of, w_k, w_t,
                        [(0, nk, False, True, True, None), (nk, nk, False, False, True, None)],
                        [(0, nq, False, True, False, qmul), (nq, nk, False, True, False, None)], name=name)

    def split_w(w, nq, nk):
        return bf(w[:, nq:]), bf(tr(jnp.concatenate([w[:, :nq], w[:, nq + nk:]], 1)))

    w_k, w_t = split_w(l0_w_qkv, D, D)
    wo = bf(l0_w_o)
    lam = l0_lam.astype(F32)
    subln = l0_subln.reshape(2 * DIFF_DH, 1)
    k, k0, v0, qt, vt = qkv_t(x_prompt, mod[0], ctx_row, w_k, w_t, D, D, False, False, "l0_proj_ctx")
    o = _flash(qt, k, vt, mode="diff", extra=(lam, subln), name="l0_attn_ctx")
    xp = _post(x_prompt, o, mod[0], ctx_row, ln_g[0], ln_b[0], wo, w1[0], w2[0], name="l0_post_ctx")
    k, qt, vt = qkv_t(x_sample, mod[0], lat_row, w_k, w_t, D, D, True, True, "l0_proj_lat")
    cache = (bf(cache_l0_k.reshape(Bl, P, D)), bf(tr(cache_l0_v.reshape(Bl, P, D))))
    o = _flash(qt, k, vt, cache, mode="diff", extra=(lam, subln), name="l0_attn_lat")
    xs = _post(x_sample, o, mod[0], lat_row, ln_g[0], ln_b[0], wo, w1[0], w2[0], name="l0_post_lat")

    w_k, w_t = split_w(l1_w_qkv, D, D)
    wo = bf(l1_w_o)
    k, k1, v1, qt, vt = qkv_t(xp, mod[1], ctx_row, w_k, w_t, D, D, False, False, "l1_proj_ctx")
    o = _band(qt, [(k, vt, S, _OWN, None)], n_slab=1, tq=S, name="l1_attn_ctx")
    xp = _post(xp, o, mod[1], ctx_row, ln_g[1], ln_b[1], wo, w1[1], w2[1], name="l1_post_ctx")
    k, qt, vt = qkv_t(xs, mod[1], lat_row, w_k, w_t, D, D, True, False, "l1_proj_lat")
    kc, vct = bf(cache_l1_k.reshape(Bl, P, D)), bf(tr(cache_l1_v.reshape(Bl, P, D)))
    tqn = NA_QROWS * GRID_W
    bias = _na_bias_tables(l1_rpb, rows)
    o = _band(qt, [(k, vt, tqn, _PREV, 0), (k, vt, tqn, _OWN, tqn), (k, vt, tqn, _NEXT, 2 * tqn),
                   (kc, vct, P, _FIRST, None)], n_slab=1, tq=tqn, bias=bias, name="l1_attn_lat")
    xs = _post(xs, o, mod[1], lat_row, ln_g[1], ln_b[1], wo, w1[1], w2[1], name="l1_post_lat")

    H, dq = MLA_HEADS, MLA_NOPE + MLA_ROPE
    wa = bf(jnp.pad(l2_w_a, ((0, 0), (0, 7 * LANES - l2_w_a.shape[1]))))
    wuq = jnp.pad(l2_w_uq.reshape(MLA_Q_RANK, H, dq), ((0, 0), (0, 0), (0, LANES - dq))).reshape(MLA_Q_RANK, H * LANES)
    wuqt = bf(tr(wuq))
    wukv = l2_w_ukv.reshape(MLA_KV_RANK, H, MLA_NOPE + MLA_VDIM)
    wk_n = jnp.pad(wukv[:, :, :MLA_NOPE], ((0, 0), (0, 0), (0, LANES - MLA_NOPE))).reshape(MLA_KV_RANK, H * LANES)
    place = np.zeros((LANES, H, LANES), np.float32)
    place[np.arange(MLA_ROPE), :, MLA_NOPE + np.arange(MLA_ROPE)] = 1.0
    wk = bf(jnp.concatenate([wk_n, jnp.asarray(place.reshape(LANES, H * LANES))], 0))
    wvt = bf(tr(wukv[:, :, MLA_NOPE:].reshape(MLA_KV_RANK, H * MLA_VDIM)))
    wo = bf(l2_w_o)
    qn, kvn = l2_q_norm.reshape(1, -1), l2_kv_norm.reshape(1, -1)
    mscale = dq ** -0.5 * LOG2E
    qt, k, vt, ckv2, kpe2 = _mla_project(xp, mod[2], ctx_row, wa, qn, kvn, wuqt, wk, wvt, mscale, name="l2_proj_ctx")
    o = _flash(qt, k, vt, mode="mla", name="l2_attn_ctx")
    xp = _post(xp, o, mod[2], ctx_row, ln_g[2], ln_b[2], wo, w1[2], w2[2], name="l2_post_ctx")
    tabs = tuple(tr(t) for t in _rope_tables(T, MLA_ROPE, MLA_NOPE, 1)) + _rope_tables(T, MLA_ROPE, 0, 1)
    qt, k, vt = _mla_project(xs, mod[2], lat_row, wa, qn, kvn, wuqt, wk, wvt, mscale, rope=tabs, name="l2_proj_lat")
    kpe_slab = jnp.pad(cache_l2_kpe, ((0, 0), (0, 0), (0, LANES - MLA_ROPE)))
    cache = _mla_cache_expand(cache_l2_ckv, kpe_slab, wk, wvt)
    o = _flash(qt, k, vt, cache, mode="mla", name="l2_attn_lat")
    xs = _post(xs, o, mod[2], lat_row, ln_g[2], ln_b[2], wo, w1[2], w2[2], name="l2_post_lat")

    nq3, nk3 = SWA_HEADS * HALF, SWA_KV_HEADS * HALF
    w = jnp.concatenate([_perm_heads(l3_w_qkv[:, :nq3], 1), l3_w_qkv[:, nq3:]], 1)
    w_k, w_t = split_w(w, nq3, nk3)
    wo = bf(_perm_heads(l3_w_o, 0))
    sink = jnp.take(l3_sink.astype(F32), jnp.asarray(_SWA_PERM)) * LOG2E

    def sink_rows(tq):
        return jnp.repeat(sink.reshape(2, 1, 8), tq, axis=-1)

    k, k3, v3, qt, vt = qkv_t(xp, mod[3], ctx_row, w_k, w_t, nq3, nk3, False, False, "l3_proj_ctx")
    o = _band(qt, [(k, vt, S, _OWN, None)], n_slab=4, tq=S, sink=sink_rows(S), name="l3_attn_ctx")
    xp = _post(xp, o, mod[3], ctx_row, ln_g[3], ln_b[3], wo, w1[3], w2[3], name="l3_post_ctx")
    k, qt, vt = qkv_t(xs, mod[3], lat_row, w_k, w_t, nq3, nk3, True, True, "l3_proj_lat")
    kc, vct = bf(cache_l3_k.reshape(Bl, P, nk3)), bf(tr(cache_l3_v.reshape(Bl, P, nk3)))
    mask = _swa_mask_tables(T // SWA_QB, 8)
    o = _band(qt, [(k, vt, SWA_QB, _PREV, 0), (k, vt, SWA_QB, _OWN, SWA_QB), (k, vt, SWA_QB, _NEXT, 2 * SWA_QB),
                   (kc, vct, P, _FIRST, None)], n_slab=4, tq=SWA_QB, bias=mask, sink=sink_rows(SWA_QB),
              name="l3_attn_lat")
    xs = _post(xs, o, mod[3], lat_row, ln_g[3], ln_b[3], wo, w1[3], w2[3], name="l3_post_lat")

    return (xp, xs,
            k0.reshape(Bc, S, DIFF_HEADS, 2 * DIFF_DH), v0.reshape(Bc, S, DIFF_HEADS, 2 * DIFF_DH),
            k1.reshape(Bc, S, NA_HEADS, HALF), v1.reshape(Bc, S, NA_HEADS, HALF),
            ckv2, kpe2,
            k3.reshape(Bc, S, SWA_KV_HEADS, HALF), v3.reshape(Bc, S, SWA_KV_HEADS, HALF))
```

```python
import functools
import math

import numpy as np
import jax
import jax.numpy as jnp
from jax import lax
from jax.experimental import pallas as pl
from jax.experimental.pallas import tpu as pltpu

D_MODEL = 1024
DEPTH = 4
GRID_W = 64
ROPE_BASE = 10000.0
LN_EPS = 1e-5
NEG_INF = -1e30
ALPHA = (2 * DEPTH) ** 0.25
D_FF = 4 * D_MODEL
DIFF_DH = 64
DIFF_HEADS = 8
DIFF_LAMBDA_INIT = 0.8 - 0.6 * math.exp(-0.3 * 0)
NA_HEADS = 16
NA_KH = 8
NA_KW = 16
NA_QROWS = 4
MLA_HEADS = 16
MLA_NOPE = 64
MLA_ROPE = 32
MLA_VDIM = 64
MLA_Q_RANK = 512
MLA_KV_RANK = 256
SWA_HEADS = 16
SWA_KV_HEADS = 4
SWA_WINDOW = 128
SWA_QB = 128
LOG2E = math.log2(math.e)

LANES = 128
HALF = LANES // 2
BF16_ROWS = 16
MOD_ROWS = 16
VMEM_LIMIT = 56 * 1024 * 1024

F32 = jnp.float32
BF16 = jnp.bfloat16


def _cparams(sem):
    return pltpu.CompilerParams(dimension_semantics=sem, vmem_limit_bytes=VMEM_LIMIT)


def _dot(a, b):
    return jnp.dot(a, b, preferred_element_type=F32)


def _layer_norm(x, g, b):
    mu = jnp.mean(x, -1, keepdims=True)
    xc = x - mu
    var = jnp.mean(xc * xc, -1, keepdims=True)
    return xc * lax.rsqrt(var + LN_EPS) * g + b


def _rms_norm(x, g, axis=-1):
    return x * lax.rsqrt(jnp.mean(x * x, axis, keepdims=True) + LN_EPS) * g


def _rope_slab(x, cos, sin_up, sin_dn, n):
    return x * cos + pltpu.roll(x, LANES - n, 1) * sin_up + pltpu.roll(x, n, 1) * sin_dn


def _rope_slab_t(x, cos, sin_up, sin_dn, n):
    up = jnp.concatenate([x[n:], x[:n]], axis=0)
    dn = jnp.concatenate([x[LANES - n:], x[:LANES - n]], axis=0)
    return x * cos + up * sin_up + dn * sin_dn


def _split_heads_t(qt):
    lo = lax.broadcasted_iota(jnp.int32, qt.shape, 0) < HALF
    zero = jnp.zeros_like(qt)
    return jnp.concatenate([jnp.where(lo, qt, zero), jnp.where(lo, zero, qt)], axis=1)


def _with_ones_row(vt):
    r = lax.broadcasted_iota(jnp.int32, (BF16_ROWS, vt.shape[1]), 0)
    return jnp.concatenate([vt, jnp.where(r == 0, 1.0, 0.0).astype(BF16)], axis=0)


def _modulated(x_ref, mod_ref):
    return x_ref[0] * (1.0 + mod_ref[0, 1:2, :]) + mod_ref[0, 0:1, :]


def _mod_kernel(c_ref, w_ref, b_ref, o_ref):
    c = c_ref[...]
    s = c / (1.0 + jnp.exp(-c))
    o_ref[0] = _dot(s.astype(BF16), w_ref[0].astype(BF16)) + b_ref[0]


def _modulation(cond, w_mod, b_mod):
    tn = 512
    n = 6 * D_MODEL
    out = pl.pallas_call(
        _mod_kernel,
        grid=(DEPTH, n // tn),
        in_specs=[pl.BlockSpec((MOD_ROWS, D_MODEL), lambda l, j: (0, 0)),
                  pl.BlockSpec((1, D_MODEL, tn), lambda l, j: (l, 0, j)),
                  pl.BlockSpec((1, 1, tn), lambda l, j: (l, 0, j))],
        out_specs=pl.BlockSpec((1, MOD_ROWS, tn), lambda l, j: (l, 0, j)),
        out_shape=jax.ShapeDtypeStruct((DEPTH, MOD_ROWS, n), F32),
        compiler_params=_cparams(("parallel", "parallel")),
        name="modulation",
    )(cond, w_mod, b_mod.reshape(DEPTH, 1, n))
    return out.reshape(DEPTH, MOD_ROWS, 6, D_MODEL)


def _proj_kernel(*refs, segs, tsegs, rope_n, cw):
    x_ref, mod_ref = refs[:2]
    pos = 2
    w_ref = wt_ref = None
    if segs:
        w_ref = refs[pos]
        pos += 1
    if tsegs:
        wt_ref = refs[pos]
        pos += 1
    if rope_n and any(s[2] for s in segs):
        cos, sup, sdn = (r[...] for r in refs[pos:pos + 3])
        pos += 3
    if rope_n and any(s[2] for s in tsegs):
        cos_t, sup_t, sdn_t = (r[...] for r in refs[pos:pos + 3])
        pos += 3
    outs = list(refs[pos:])
    h32 = _modulated(x_ref, mod_ref)
    h = h32.astype(BF16)
    for (start, width, rope, want16, want32, mul) in segs:
        o16 = outs.pop(0) if want16 else None
        o32 = outs.pop(0) if want32 else None
        for c in range(0, width, cw):
            acc = _dot(h, w_ref[:, start + c:start + c + cw])
            if want32:
                o32[0, :, c:c + cw] = acc
            if mul is not None:
                acc = acc * mul
            if rope:
                acc = jnp.concatenate(
                    [_rope_slab(acc[:, s:s + LANES], cos, sup, sdn, rope_n) for s in range(0, cw, LANES)], axis=1)
            if want16:
                o16[0, :, c:c + cw] = acc.astype(BF16)
    if tsegs:
        ht = h32.T.astype(BF16)
        for (start, width, rope, _, _, mul) in tsegs:
            o16 = outs.pop(0)
            for c in range(0, width, cw):
                acc = _dot(wt_ref[start + c:start + c + cw, :], ht)
                if mul is not None:
                    acc = acc * mul
                if rope:
                    acc = jnp.concatenate(
                        [_rope_slab_t(acc[s:s + LANES], cos_t, sup_t, sdn_t, rope_n) for s in range(0, cw, LANES)],
                        axis=0)
                o16[0, c:c + cw, :] = acc.astype(BF16)


def _project(x, mod, row_of, w, wt, segs, tsegs, rope=None, rope_t=None, tm=512, cw=256, name="proj"):
    B, T, _ = x.shape
    tm = min(tm, T)
    const = lambda b, i: (0, 0)
    in_specs = [pl.BlockSpec((1, tm, D_MODEL), lambda b, i: (b, i, 0)),
                pl.BlockSpec((1, 6, D_MODEL), lambda b, i: (row_of(b), 0, 0))]
    args = [x, mod]
    if segs:
        in_specs.append(pl.BlockSpec(w.shape, const))
        args.append(w)
    if tsegs:
        in_specs.append(pl.BlockSpec(wt.shape, const))
        args.append(wt)
    rope_n = 0
    if rope is not None:
        rope_n, tabs = rope
        in_specs += [pl.BlockSpec((tm, LANES), lambda b, i: (i, 0))] * 3
        args += list(tabs)
    if rope_t is not None:
        rope_n, tabs = rope_t
        in_specs += [pl.BlockSpec((LANES, tm), lambda b, i: (0, i))] * 3
        args += list(tabs)
    out_specs, out_shape = [], []
    for (_, width, _, want16, want32, _) in segs:
        for want, dt in ((want16, BF16), (want32, F32)):
            if want:
                out_specs.append(pl.BlockSpec((1, tm, width), lambda b, i: (b, i, 0)))
                out_shape.append(jax.ShapeDtypeStruct((B, T, width), dt))
    for (_, width, _, _, _, _) in tsegs:
        out_specs.append(pl.BlockSpec((1, width, tm), lambda b, i: (b, 0, i)))
        out_shape.append(jax.ShapeDtypeStruct((B, width, T), BF16))
    return pl.pallas_call(
        functools.partial(_proj_kernel, segs=tuple(segs), tsegs=tuple(tsegs), rope_n=rope_n, cw=cw),
        grid=(B, T // tm), in_specs=in_specs, out_specs=out_specs, out_shape=out_shape,
        compiler_params=_cparams(("parallel", "parallel")), name=name,
    )(*args)


def _mla_proj_kernel(*refs, lat, scale, cw):
    x_ref, mod_ref, wa_ref, qn_ref, kvn_ref, wuqt_ref, wk_ref, wvt_ref = refs[:8]
    pos = 8
    if lat:
        qcos, qsup, qsdn, kcos, ksup, ksdn = (r[...] for r in refs[8:14])
        pos = 14
    qt_out, k_out, vt_out = refs[pos:pos + 3]
    h = _modulated(x_ref, mod_ref).astype(BF16)
    a = _dot(h, wa_ref[...])
    cq = _rms_norm(a[:, :MLA_Q_RANK], qn_ref[...])
    ckv = _rms_norm(a[:, MLA_Q_RANK:MLA_Q_RANK + MLA_KV_RANK], kvn_ref[...])
    kpe = a[:, MLA_Q_RANK + MLA_KV_RANK:]
    if lat:
        kpe = _rope_slab(kpe, kcos, ksup, ksdn, MLA_ROPE // 4)
    else:
        ckv_state, kpe_state = refs[pos + 3:pos + 5]
        ckv_state[0] = ckv
        kpe_state[0] = kpe[:, :MLA_ROPE]
    cqt = cq.T.astype(BF16)
    n_q = wuqt_ref.shape[0]
    for c in range(0, n_q, cw):
        q = _dot(wuqt_ref[c:c + cw, :], cqt) * scale
        if lat:
            q = jnp.concatenate(
                [_rope_slab_t(q[s:s + LANES], qcos, qsup, qsdn, MLA_ROPE // 4) for s in range(0, cw, LANES)], axis=0)
        qt_out[0, c:c + cw, :] = q.astype(BF16)
    kin = jnp.concatenate([ckv.astype(BF16), kpe.astype(BF16)], axis=1)
    for c in range(0, wk_ref.shape[1], cw):
        k_out[0, :, c:c + cw] = _dot(kin, wk_ref[:, c:c + cw]).astype(BF16)
    ckvt = ckv.T.astype(BF16)
    for c in range(0, wvt_ref.shape[0], cw):
        vt_out[0, c:c + cw, :] = _dot(wvt_ref[c:c + cw, :], ckvt).astype(BF16)


def _mla_project(x, mod, row_of, wa, qn, kvn, wuqt, wk, wvt, scale, rope=None, tm=512, name="mla_proj"):
    B, T, _ = x.shape
    tm = min(tm, T)
    lat = rope is not None
    const = lambda b, i: (0, 0)
    in_specs = [pl.BlockSpec((1, tm, D_MODEL), lambda b, i: (b, i, 0)),
                pl.BlockSpec((1, 6, D_MODEL), lambda b, i: (row_of(b), 0, 0))]
    in_specs += [pl.BlockSpec(a.shape, const) for a in (wa, qn, kvn, wuqt, wk, wvt)]
    args = [x, mod, wa, qn, kvn, wuqt, wk, wvt]
    if lat:
        in_specs += [pl.BlockSpec((LANES, tm), lambda b, i: (0, i))] * 3
        in_specs += [pl.BlockSpec((tm, LANES), lambda b, i: (i, 0))] * 3
        args += list(rope)
    tok = lambda b, i: (b, i, 0)
    feat = lambda b, i: (b, 0, i)
    out_specs = [pl.BlockSpec((1, wuqt.shape[0], tm), feat), pl.BlockSpec((1, tm, wk.shape[1]), tok),
                 pl.BlockSpec((1, wvt.shape[0], tm), feat)]
    out_shape = [jax.ShapeDtypeStruct((B, wuqt.shape[0], T), BF16), jax.ShapeDtypeStruct((B, T, wk.shape[1]), BF16),
                 jax.ShapeDtypeStruct((B, wvt.shape[0], T), BF16)]
    if not lat:
        out_specs += [pl.BlockSpec((1, tm, MLA_KV_RANK), tok), pl.BlockSpec((1, tm, MLA_ROPE), tok)]
        out_shape += [jax.ShapeDtypeStruct((B, T, MLA_KV_RANK), F32), jax.ShapeDtypeStruct((B, T, MLA_ROPE), F32)]
    return pl.pallas_call(
        functools.partial(_mla_proj_kernel, lat=lat, scale=scale, cw=512),
        grid=(B, T // tm), in_specs=in_specs, out_specs=out_specs, out_shape=out_shape,
        compiler_params=_cparams(("parallel", "parallel")), name=name,
    )(*args)


def _mla_cache_kernel(ckv_ref, kpe_ref, wk_ref, wvt_ref, k_out, vt_out):
    ckv = ckv_ref[0]
    kin = jnp.concatenate([ckv.astype(BF16), kpe_ref[0].astype(BF16)], axis=1)
    k_out[0] = _dot(kin, wk_ref[...]).astype(BF16)
    vt_out[0] = _dot(wvt_ref[...], ckv.T.astype(BF16)).astype(BF16)


def _mla_cache_expand(c_ckv, c_kpe_slab, wk, wvt):
    B, P, _ = c_ckv.shape
    const = lambda b: (0, 0)
    return pl.pallas_call(
        _mla_cache_kernel, grid=(B,),
        in_specs=[pl.BlockSpec((1, P, MLA_KV_RANK), lambda b: (b, 0, 0)),
                  pl.BlockSpec((1, P, LANES), lambda b: (b, 0, 0)),
                  pl.BlockSpec(wk.shape, const), pl.BlockSpec(wvt.shape, const)],
        out_specs=[pl.BlockSpec((1, P, wk.shape[1]), lambda b: (b, 0, 0)),
                   pl.BlockSpec((1, wvt.shape[0], P), lambda b: (b, 0, 0))],
        out_shape=[jax.ShapeDtypeStruct((B, P, wk.shape[1]), BF16),
                   jax.ShapeDtypeStruct((B, wvt.shape[0], P), BF16)],
        compiler_params=_cparams(("parallel",)), name="mla_cache_expand",
    )(c_ckv, c_kpe_slab, wk, wvt)


def _flash_kernel(*refs, mode, has_cache, gpb, tq, tk):
    qt_ref = refs[0]
    pos = 1
    if has_cache:
        kc_ref, vct_ref = refs[1:3]
        pos = 3
    k_ref, vt_ref = refs[pos:pos + 2]
    pos += 2
    if mode == "diff":
        lam_ref, subln_ref = refs[pos:pos + 2]
        pos += 2
    o_ref, s_sc, acc_sc = refs[pos:pos + 3]
    dv = LANES if mode == "diff" else HALF
    wq = LANES if mode == "diff" else 2 * LANES

    blocks = []
    if has_cache:
        blocks += [(kc_ref, vct_ref, j) for j in range(kc_ref.shape[1] // tk)]
    blocks += [(k_ref, vt_ref, j) for j in range(k_ref.shape[1] // tk)]

    qts = [qt_ref[0, g * wq:(g + 1) * wq, :] for g in range(gpb)]
    if mode == "diff":
        qts = [_split_heads_t(qt) for qt in qts]

    def logits(n, g):
        kr, _, j = blocks[n]
        k = kr[0, j * tk:(j + 1) * tk, g * wq:(g + 1) * wq]
        buf = 2 * g + n % 2
        if mode == "diff":
            s_sc[buf] = _dot(k, qts[g])
        else:
            s_sc[buf, :, :tq] = _dot(k[:, :LANES], qts[g][:LANES])
            s_sc[buf, :, tq:] = _dot(k[:, LANES:], qts[g][LANES:])

    for g in range(gpb):
        logits(0, g)
    m = [None] * gpb
    for n in range(len(blocks)):
        for g in range(gpb):
            if n + 1 < len(blocks):
                logits(n + 1, g)
            _, vr, j = blocks[n]
            vt = vr[0, g * LANES:(g + 1) * LANES, j * tk:(j + 1) * tk]
            s = s_sc[2 * g + n % 2]
            m_cur = jnp.max(s, 0, keepdims=True)
            m_new = m_cur if m[g] is None else jnp.maximum(m[g], m_cur)
            pb = jnp.exp2(s - m_new).astype(BF16)
            if mode == "diff":
                pv = _dot(_with_ones_row(vt), pb)
            else:
                pv = jnp.concatenate([_dot(_with_ones_row(vt[:HALF]), pb[:, :tq]),
                                      _dot(_with_ones_row(vt[HALF:]), pb[:, tq:])], axis=1)
            if m[g] is None:
                acc_sc[g] = pv
            else:
                acc_sc[g] = jnp.exp2(m[g] - m_new) * acc_sc[g] + pv
            m[g] = m_new

    if mode == "diff":
        lam = lam_ref[...]
        lam_full = (jnp.exp(jnp.sum(lam[0:1] * lam[1:2], keepdims=True))
                    - jnp.exp(jnp.sum(lam[2:3] * lam[3:4], keepdims=True)) + DIFF_LAMBDA_INIT)
    for g in range(gpb):
        acc = acc_sc[g]
        o = acc[:dv] / acc[dv:dv + 1]
        o0, o1 = o[:, :tq], o[:, tq:]
        if mode == "diff":
            out_t = _rms_norm(o0 - lam_full * o1, subln_ref[...], axis=0) * (1.0 - DIFF_LAMBDA_INIT)
        else:
            out_t = jnp.concatenate([o0, o1], axis=0)
        o_ref[0, :, g * LANES:(g + 1) * LANES] = out_t.T.astype(o_ref.dtype)


def _flash(qt, k, vt, cache=None, *, mode, extra=(), gpb=1, tq=1024, tk=512, name="flash"):
    B, _, T = qt.shape
    S = k.shape[1]
    wq = LANES if mode == "diff" else 2 * LANES
    G = vt.shape[1] // LANES
    tq, tk = min(tq, T), min(tk, S)
    in_specs = [pl.BlockSpec((1, gpb * wq, tq), lambda b, g, i: (b, g, i))]
    args = [qt]
    if cache is not None:
        kc, vct = cache
        P = kc.shape[1]
        in_specs += [pl.BlockSpec((1, P, gpb * wq), lambda b, g, i: (b, 0, g)),
                     pl.BlockSpec((1, gpb * LANES, P), lambda b, g, i: (b, g, 0))]
        args += [kc, vct]
    in_specs += [pl.BlockSpec((1, S, gpb * wq), lambda b, g, i: (b, 0, g)),
                 pl.BlockSpec((1, gpb * LANES, S), lambda b, g, i: (b, g, 0))]
    args += [k, vt]
    for e in extra:
        in_specs.append(pl.BlockSpec(e.shape, lambda b, g, i: (0, 0)))
        args.append(e)
    dv = LANES if mode == "diff" else HALF
    return pl.pallas_call(
        functools.partial(_flash_kernel, mode=mode, has_cache=cache is not None, gpb=gpb, tq=tq, tk=tk),
        grid=(B, G // gpb, T // tq), in_specs=in_specs,
        out_specs=pl.BlockSpec((1, tq, gpb * LANES), lambda b, g, i: (b, i, g)),
        out_shape=jax.ShapeDtypeStruct((B, T, G * LANES), BF16),
        scratch_shapes=[pltpu.VMEM((2 * gpb, tk, 2 * tq), F32), pltpu.VMEM((gpb, dv + BF16_ROWS, 2 * tq), F32)],
        compiler_params=_cparams(("parallel", "parallel", "parallel")), name=name,
    )(*args)


def _band_kernel(*refs, n_seg, seg_bias, n_slab, gpb, has_bias, bias_groups, has_sink, tq):
    qt_ref = refs[0]
    kv = refs[1:1 + 2 * n_seg]
    pos = 1 + 2 * n_seg
    bias_ref = sink_ref = None
    if has_bias:
        bias_ref = refs[pos]
        pos += 1
    if has_sink:
        sink_ref = refs[pos]
        pos += 1
    o_ref = refs[pos]
    for g in range(gpb):
        q0 = g * n_slab * LANES
        qst = jnp.concatenate(
            [_split_heads_t(qt_ref[0, q0 + s * LANES:q0 + (s + 1) * LANES, :]) for s in range(n_slab)], axis=1)
        logits = []
        for i in range(n_seg):
            lg = _dot(kv[2 * i][0, :, g * LANES:(g + 1) * LANES], qst)
            if seg_bias[i] is not None:
                lg = lg + bias_ref[0, g if bias_groups > 1 else 0, seg_bias[i]:seg_bias[i] + lg.shape[0], :]
            logits.append(lg)
        m = functools.reduce(jnp.maximum, [jnp.max(lg, 0, keepdims=True) for lg in logits])
        if has_sink:
            sink = sink_ref[g]
            m = jnp.maximum(m, sink)
        acc = None
        for i in range(n_seg):
            vt = kv[2 * i + 1][0, g * LANES:(g + 1) * LANES, :]
            pv = _dot(_with_ones_row(vt), jnp.exp2(logits[i] - m).astype(BF16))
            acc = pv if acc is None else acc + pv
        l = acc[LANES:LANES + 1]
        if has_sink:
            l = l + jnp.exp2(sink - m)
        o = acc[:LANES] / l
        for s in range(n_slab):
            c = 2 * s * tq
            out_t = jnp.concatenate([o[:HALF, c:c + tq], o[HALF:, c + tq:c + 2 * tq]], axis=0)
            o_ref[0, :, q0 + s * LANES:q0 + (s + 1) * LANES] = out_t.T.astype(o_ref.dtype)


def _band(qt, segs, *, n_slab, tq, gpb=1, bias=None, sink=None, name="band"):
    B, _, T = qt.shape
    G = qt.shape[1] // (n_slab * LANES)
    nq = T // tq
    R = n_slab * 2 * tq
    in_specs = [pl.BlockSpec((1, gpb * n_slab * LANES, tq), lambda g, b, i: (b, g, i))]
    args = [qt]
    seg_bias = []
    for (k, vt, ts, blk_of, boff) in segs:
        in_specs += [pl.BlockSpec((1, ts, gpb * LANES), (lambda f: lambda g, b, i: (b, f(i, nq), g))(blk_of)),
                     pl.BlockSpec((1, gpb * LANES, ts), (lambda f: lambda g, b, i: (b, g, f(i, nq)))(blk_of))]
        args += [k, vt]
        seg_bias.append(boff)
    bias_groups = 1
    if bias is not None:
        n_var, bias_groups, sb, _ = bias.shape
        var_of = lambda i: jnp.where(i == 0, 0, jnp.where(i == nq - 1, n_var - 1, 1))
        nb = gpb if bias_groups > 1 else 1
        in_specs.append(pl.BlockSpec((1, nb, sb, R), lambda g, b, i: (var_of(i), g if bias_groups > 1 else 0, 0, 0)))
        args.append(bias)
    if sink is not None:
        in_specs.append(pl.BlockSpec((gpb, 1, R), lambda g, b, i: (g, 0, 0)))
        args.append(sink)
    return pl.pallas_call(
        functools.partial(_band_kernel, n_seg=len(segs), seg_bias=tuple(seg_bias), n_slab=n_slab, gpb=gpb,
                          has_bias=bias is not None, bias_groups=bias_groups, has_sink=sink is not None, tq=tq),
        grid=(G // gpb, B, nq), in_specs=in_specs,
        out_specs=pl.BlockSpec((1, tq, gpb * n_slab * LANES), lambda g, b, i: (b, i, g)),
        out_shape=jax.ShapeDtypeStruct((B, T, G * n_slab * LANES), BF16),
        compiler_params=_cparams(("parallel", "parallel", "parallel")), name=name,
    )(*args)


_OWN = lambda i, nq: i
_PREV = lambda i, nq: jnp.maximum(i - 1, 0)
_NEXT = lambda i, nq: jnp.minimum(i + 1, nq - 1)
_FIRST = lambda i, nq: 0


def _post_kernel(x_ref, o_ref, mod_ref, g_ref, b_ref, wo_ref, w1_ref, w2_ref, out_ref, *, cf):
    x = x_ref[0]
    y = _dot(o_ref[0], wo_ref[...])
    x1 = _layer_norm(ALPHA * x + mod_ref[0, 2:3, :] * y, g_ref[0:1, :], b_ref[0:1, :])
    h = (x1 * (1.0 + mod_ref[0, 4:5, :]) + mod_ref[0, 3:4, :]).astype(BF16)
    acc = jnp.zeros(x.shape, F32)
    for c in range(0, D_FF, cf):
        u = jnp.maximum(_dot(h, w1_ref[:, c:c + cf]), 0.0)
        acc = acc + _dot((u * u).astype(BF16), w2_ref[c:c + cf, :])
    out_ref[0] = _layer_norm(ALPHA * x1 + mod_ref[0, 5:6, :] * acc, g_ref[1:2, :], b_ref[1:2, :])


def _post(x, o, mod, row_of, ln_g, ln_b, wo, w1, w2, tm=256, name="post"):
    B, T, _ = x.shape
    tm = min(tm, T)
    const = lambda b, i: (0, 0)
    once = pl.Buffered(1)
    return pl.pallas_call(
        functools.partial(_post_kernel, cf=1024),
        grid=(B, T // tm),
        in_specs=[pl.BlockSpec((1, tm, D_MODEL), lambda b, i: (b, i, 0)),
                  pl.BlockSpec((1, tm, D_MODEL), lambda b, i: (b, i, 0)),
                  pl.BlockSpec((1, 6, D_MODEL), lambda b, i: (row_of(b), 0, 0)),
                  pl.BlockSpec((2, D_MODEL), const), pl.BlockSpec((2, D_MODEL), const),
                  pl.BlockSpec(wo.shape, const, pipeline_mode=once),
                  pl.BlockSpec(w1.shape, const, pipeline_mode=once),
                  pl.BlockSpec(w2.shape, const, pipeline_mode=once)],
        out_specs=pl.BlockSpec((1, tm, D_MODEL), lambda b, i: (b, i, 0)),
        out_shape=jax.ShapeDtypeStruct(x.shape, F32),
        compiler_params=_cparams(("parallel", "parallel")), name=name,
    )(x, o, mod, ln_g, ln_b, wo, w1, w2)


def _rope_tables(T, R, lead, reps):
    n = R // 4
    t = jnp.arange(T)
    inv = ROPE_BASE ** (-jnp.arange(n, dtype=jnp.float32) / n)

    def cs(p):
        ang = p.astype(jnp.float32)[:, None] * inv
        return jnp.cos(ang), jnp.sin(ang)

    cr, sr = cs(t // GRID_W)
    cc, sc = cs(t % GRID_W)
    z = jnp.zeros_like(sr)
    cos = jnp.concatenate([cr, cr, cc, cc], -1)
    sup = jnp.concatenate([-sr, z, -sc, z], -1)
    sdn = jnp.concatenate([z, sr, z, sc], -1)
    tail = LANES - lead - reps * R

    def slab(a, fill):
        parts = [jnp.full((T, lead), fill, F32)] if lead else []
        parts += [a] * reps
        if tail:
            parts.append(jnp.full((T, tail), fill, F32))
        return jnp.concatenate(parts, -1)

    return slab(cos, 1.0), slab(sup, 0.0), slab(sdn, 0.0)


def _na_bias_tables(rpb, rows):
    qr, kb_n = NA_QROWS, 3 * NA_QROWS
    n_dr, n_dc = 2 * NA_KH - 1, 2 * NA_KW - 1
    c = np.arange(GRID_W)
    c0 = np.clip(c - NA_KW // 2, 0, GRID_W - NA_KW)
    kc = np.arange(GRID_W)[:, None]
    col_ok = (kc >= c0[None]) & (kc < c0[None] + NA_KW)
    sel_c = ((np.arange(n_dc)[:, None, None] == (kc - c[None] + NA_KW - 1)[None]) & col_ok[None])
    sel_r = np.zeros((3, n_dr, kb_n, qr), bool)
    for v, r_base in enumerate((0, qr, rows - qr)):
        r = r_base + np.arange(qr)[None]
        kr = r_base - qr + np.arange(kb_n)[:, None]
        r0 = np.clip(r - NA_KH // 2, 0, rows - NA_KH)
        row_ok = (kr >= r0) & (kr < r0 + NA_KH)
        sel_r[v] = (np.arange(n_dr)[:, None, None] == (kr - r + NA_KH - 1)[None]) & row_ok[None]
    hp = lax.Precision.HIGHEST
    t1 = jnp.einsum("hde,ejc->hdjc", rpb.astype(F32), jnp.asarray(sel_c, F32), precision=hp)
    tab = jnp.einsum("vdkr,hdjc->vhkjrc", jnp.asarray(sel_r, F32), t1, precision=hp)
    ok = np.einsum("vdkr,ejc->vkjrc", sel_r.astype(np.int32), sel_c.astype(np.int32)) > 0
    tab = jnp.where(jnp.asarray(ok)[:, None], tab * LOG2E, NEG_INF)
    H = rpb.shape[0]
    tab = tab.reshape(3, H // 2, 2, kb_n * GRID_W, qr * GRID_W)
    return jnp.swapaxes(tab, 2, 3).reshape(3, H // 2, kb_n * GRID_W, 2 * qr * GRID_W)


def _swa_mask_tables(nb, reps):
    qi = np.arange(SWA_QB)[None, :]
    kj = np.arange(3 * SWA_QB)[:, None]
    tabs = []
    for blk in (0, 1, nb - 1):
        kpos = (blk - 1) * SWA_QB + kj
        ok = (np.abs(kj - SWA_QB - qi) <= SWA_WINDOW) & (kpos >= 0) & (kpos < nb * SWA_QB)
        tabs.append(np.tile(np.where(ok, 0.0, NEG_INF).astype(np.float32), (1, reps))[None])
    return jnp.asarray(np.stack(tabs, 0))


_SWA_PERM = np.array([8 * p + 4 * half + s for p in range(2) for s in range(4) for half in range(2)])


def _perm_heads(w, axis, dh=HALF):
    shape = w.shape
    w = w.reshape(shape[:axis] + (SWA_HEADS, dh) + shape[axis + 1:])
    w = jnp.take(w, jnp.asarray(_SWA_PERM), axis=axis)
    return w.reshape(shape)


def kernel(x_prompt, x_sample, cache_l0_k, cache_l0_v, cache_l1_k, cache_l1_v, cache_l2_ckv, cache_l2_kpe, cache_l3_k, cache_l3_v, c, c_ctx, w_mod, b_mod, ln_g, ln_b, w_mlp1, w_mlp2, l0_w_qkv, l0_lam, l0_subln, l0_w_o, l1_w_qkv, l1_rpb, l1_w_o, l2_w_a, l2_q_norm, l2_kv_norm, l2_w_uq, l2_w_ukv, l2_w_o, l3_w_qkv, l3_sink, l3_w_o):
    D = D_MODEL
    Bc, S, _ = x_prompt.shape
    Bl, T, _ = x_sample.shape
    P = cache_l0_k.shape[1]
    assert 1 + Bl <= MOD_ROWS and T % GRID_W == 0
    rows = T // GRID_W

    cond = jnp.concatenate([c_ctx[None], c, jnp.zeros((MOD_ROWS - 1 - Bl, D), F32)], 0)
    mod = _modulation(cond, w_mod, b_mod)
    ctx_row = lambda b: 0
    lat_row = lambda b: b + 1

    bf = lambda a: a.astype(BF16)
    tr = lambda a: jnp.swapaxes(a, -1, -2)
    w1 = bf(w_mlp1)
    w2 = bf(w_mlp2)
    qmul = HALF ** -0.5 * LOG2E

    tabs64 = _rope_tables(T, DIFF_DH, 0, 2)
    rope64 = (DIFF_DH // 4, tabs64)
    rope64_t = (DIFF_DH // 4, tuple(tr(t) for t in tabs64))

    def qkv_t(x, mod_l, row_of, w_k, w_t, nq, nk, lat, rope, name):
        if lat:
            return _project(x, mod_l, row_of, w_k, w_t, [(0, nk, rope, True, False, None)],
                            [(0, nq, rope, True, False, qmul), (nq, nk, False, True, False, None)],
                            rope=rope64 if rope else None, rope_t=rope64_t if rope else None, name=name)
        return _project(x, mod_l, row_of, w_k, w_t,
                        [(0, nk, False, True, True, None), (nk, nk, False, False, True, None)],
                        [(0, nq, False, True, False, qmul), (nq, nk, False, True, False, None)], name=name)

    def split_w(w, nq, nk):
        return bf(w[:, nq:]), bf(tr(jnp.concatenate([w[:, :nq], w[:, nq + nk:]], 1)))

    w_k, w_t = split_w(l0_w_qkv, D, D)
    wo = bf(l0_w_o)
    lam = l0_lam.astype(F32)
    subln = l0_subln.reshape(2 * DIFF_DH, 1)
    k, k0, v0, qt, vt = qkv_t(x_prompt, mod[0], ctx_row, w_k, w_t, D, D, False, False, "l0_proj_ctx")
    o = _flash(qt, k, vt, mode="diff", extra=(lam, subln), gpb=DIFF_HEADS, name="l0_attn_ctx")
    xp = _post(x_prompt, o, mod[0], ctx_row, ln_g[0], ln_b[0], wo, w1[0], w2[0], name="l0_post_ctx")
    k, qt, vt = qkv_t(x_sample, mod[0], lat_row, w_k, w_t, D, D, True, True, "l0_proj_lat")
    cache = (bf(cache_l0_k.reshape(Bl, P, D)), bf(tr(cache_l0_v.reshape(Bl, P, D))))
    o = _flash(qt, k, vt, cache, mode="diff", extra=(lam, subln), name="l0_attn_lat")
    xs = _post(x_sample, o, mod[0], lat_row, ln_g[0], ln_b[0], wo, w1[0], w2[0], name="l0_post_lat")

    w_k, w_t = split_w(l1_w_qkv, D, D)
    wo = bf(l1_w_o)
    k, k1, v1, qt, vt = qkv_t(xp, mod[1], ctx_row, w_k, w_t, D, D, False, False, "l1_proj_ctx")
    o = _band(qt, [(k, vt, S, _OWN, None)], n_slab=1, tq=S, gpb=NA_HEADS // 2, name="l1_attn_ctx")
    xp = _post(xp, o, mod[1], ctx_row, ln_g[1], ln_b[1], wo, w1[1], w2[1], name="l1_post_ctx")
    k, qt, vt = qkv_t(xs, mod[1], lat_row, w_k, w_t, D, D, True, False, "l1_proj_lat")
    kc, vct = bf(cache_l1_k.reshape(Bl, P, D)), bf(tr(cache_l1_v.reshape(Bl, P, D)))
    tqn = NA_QROWS * GRID_W
    bias = _na_bias_tables(l1_rpb, rows)
    o = _band(qt, [(k, vt, tqn, _PREV, 0), (k, vt, tqn, _OWN, tqn), (k, vt, tqn, _NEXT, 2 * tqn),
                   (kc, vct, P, _FIRST, None)], n_slab=1, tq=tqn, gpb=4, bias=bias, name="l1_attn_lat")
    xs = _post(xs, o, mod[1], lat_row, ln_g[1], ln_b[1], wo, w1[1], w2[1], name="l1_post_lat")

    H, dq = MLA_HEADS, MLA_NOPE + MLA_ROPE
    wa = bf(jnp.pad(l2_w_a, ((0, 0), (0, 7 * LANES - l2_w_a.shape[1]))))
    wuq = jnp.pad(l2_w_uq.reshape(MLA_Q_RANK, H, dq), ((0, 0), (0, 0), (0, LANES - dq))).reshape(MLA_Q_RANK, H * LANES)
    wuqt = bf(tr(wuq))
    wukv = l2_w_ukv.reshape(MLA_KV_RANK, H, MLA_NOPE + MLA_VDIM)
    wk_n = jnp.pad(wukv[:, :, :MLA_NOPE], ((0, 0), (0, 0), (0, LANES - MLA_NOPE))).reshape(MLA_KV_RANK, H * LANES)
    place = np.zeros((LANES, H, LANES), np.float32)
    place[np.arange(MLA_ROPE), :, MLA_NOPE + np.arange(MLA_ROPE)] = 1.0
    wk = bf(jnp.concatenate([wk_n, jnp.asarray(place.reshape(LANES, H * LANES))], 0))
    wvt = bf(tr(wukv[:, :, MLA_NOPE:].reshape(MLA_KV_RANK, H * MLA_VDIM)))
    wo = bf(l2_w_o)
    qn, kvn = l2_q_norm.reshape(1, -1), l2_kv_norm.reshape(1, -1)
    mscale = dq ** -0.5 * LOG2E
    qt, k, vt, ckv2, kpe2 = _mla_project(xp, mod[2], ctx_row, wa, qn, kvn, wuqt, wk, wvt, mscale, name="l2_proj_ctx")
    o = _flash(qt, k, vt, mode="mla", gpb=MLA_HEADS // 2, name="l2_attn_ctx")
    xp = _post(xp, o, mod[2], ctx_row, ln_g[2], ln_b[2], wo, w1[2], w2[2], name="l2_post_ctx")
    tabs = tuple(tr(t) for t in _rope_tables(T, MLA_ROPE, MLA_NOPE, 1)) + _rope_tables(T, MLA_ROPE, 0, 1)
    qt, k, vt = _mla_project(xs, mod[2], lat_row, wa, qn, kvn, wuqt, wk, wvt, mscale, rope=tabs, name="l2_proj_lat")
    kpe_slab = jnp.pad(cache_l2_kpe, ((0, 0), (0, 0), (0, LANES - MLA_ROPE)))
    cache = _mla_cache_expand(cache_l2_ckv, kpe_slab, wk, wvt)
    o = _flash(qt, k, vt, cache, mode="mla", name="l2_attn_lat")
    xs = _post(xs, o, mod[2], lat_row, ln_g[2], ln_b[2], wo, w1[2], w2[2], name="l2_post_lat")

    nq3, nk3 = SWA_HEADS * HALF, SWA_KV_HEADS * HALF
    w = jnp.concatenate([_perm_heads(l3_w_qkv[:, :nq3], 1), l3_w_qkv[:, nq3:]], 1)
    w_k, w_t = split_w(w, nq3, nk3)
    wo = bf(_perm_heads(l3_w_o, 0))
    sink = jnp.take(l3_sink.astype(F32), jnp.asarray(_SWA_PERM)) * LOG2E

    def sink_rows(tq):
        return jnp.repeat(sink.reshape(2, 1, 8), tq, axis=-1)

    k, k3, v3, qt, vt = qkv_t(xp, mod[3], ctx_row, w_k, w_t, nq3, nk3, False, False, "l3_proj_ctx")
    o = _band(qt, [(k, vt, S, _OWN, None)], n_slab=4, tq=S, gpb=2, sink=sink_rows(S), name="l3_attn_ctx")
    xp = _post(xp, o, mod[3], ctx_row, ln_g[3], ln_b[3], wo, w1[3], w2[3], name="l3_post_ctx")
    k, qt, vt = qkv_t(xs, mod[3], lat_row, w_k, w_t, nq3, nk3, True, True, "l3_proj_lat")
    kc, vct = bf(cache_l3_k.reshape(Bl, P, nk3)), bf(tr(cache_l3_v.reshape(Bl, P, nk3)))
    mask = _swa_mask_tables(T // SWA_QB, 8)
    o = _band(qt, [(k, vt, SWA_QB, _PREV, 0), (k, vt, SWA_QB, _OWN, SWA_QB), (k, vt, SWA_QB, _NEXT, 2 * SWA_QB),
                   (kc, vct, P, _FIRST, None)], n_slab=4, tq=SWA_QB, gpb=2, bias=mask, sink=sink_rows(SWA_QB),
              name="l3_attn_lat")
    xs = _post(xs, o, mod[3], lat_row, ln_g[3], ln_b[3], wo, w1[3], w2[3], name="l3_post_lat")

    return (xp, xs,
            k0.reshape(Bc, S, DIFF_HEADS, 2 * DIFF_DH), v0.reshape(Bc, S, DIFF_HEADS, 2 * DIFF_DH),
            k1.reshape(Bc, S, NA_HEADS, HALF), v1.reshape(Bc, S, NA_HEADS, HALF),
            ckv2, kpe2,
            k3.reshape(Bc, S, SWA_KV_HEADS, HALF), v3.reshape(Bc, S, SWA_KV_HEADS, HALF))
```

```python
import functools
import math

import numpy as np
import jax
import jax.numpy as jnp
from jax import lax
from jax.experimental import pallas as pl
from jax.experimental.pallas import tpu as pltpu

D_MODEL = 1024
DEPTH = 4
GRID_W = 64
ROPE_BASE = 10000.0
LN_EPS = 1e-5
NEG_INF = -1e30
ALPHA = (2 * DEPTH) ** 0.25
D_FF = 4 * D_MODEL
DIFF_DH = 64
DIFF_HEADS = 8
DIFF_LAMBDA_INIT = 0.8 - 0.6 * math.exp(-0.3 * 0)
NA_HEADS = 16
NA_KH = 8
NA_KW = 16
NA_QROWS = 4
MLA_HEADS = 16
MLA_NOPE = 64
MLA_ROPE = 32
MLA_VDIM = 64
MLA_Q_RANK = 512
MLA_KV_RANK = 256
SWA_HEADS = 16
SWA_KV_HEADS = 4
SWA_WINDOW = 128
SWA_QB = 128
LOG2E = math.log2(math.e)

LANES = 128
HALF = LANES // 2
BF16_ROWS = 16
MOD_ROWS = 16
VMEM_LIMIT = 56 * 1024 * 1024

F32 = jnp.float32
BF16 = jnp.bfloat16


def _cparams(sem):
    return pltpu.CompilerParams(dimension_semantics=sem, vmem_limit_bytes=VMEM_LIMIT)


def _dot(a, b):
    return jnp.dot(a, b, preferred_element_type=F32)


def _layer_norm(x, g, b):
    mu = jnp.mean(x, -1, keepdims=True)
    xc = x - mu
    var = jnp.mean(xc * xc, -1, keepdims=True)
    return xc * lax.rsqrt(var + LN_EPS) * g + b


def _rms_norm(x, g, axis=-1):
    return x * lax.rsqrt(jnp.mean(x * x, axis, keepdims=True) + LN_EPS) * g


def _rope_slab(x, cos, sin_up, sin_dn, n):
    return x * cos + pltpu.roll(x, LANES - n, 1) * sin_up + pltpu.roll(x, n, 1) * sin_dn


def _rope_slab_t(x, cos, sin_up, sin_dn, n):
    up = jnp.concatenate([x[n:], x[:n]], axis=0)
    dn = jnp.concatenate([x[LANES - n:], x[:LANES - n]], axis=0)
    return x * cos + up * sin_up + dn * sin_dn


def _split_heads_t(qt):
    lo = lax.broadcasted_iota(jnp.int32, qt.shape, 0) < HALF
    zero = jnp.zeros_like(qt)
    return jnp.concatenate([jnp.where(lo, qt, zero), jnp.where(lo, zero, qt)], axis=1)


def _with_ones_row(vt):
    r = lax.broadcasted_iota(jnp.int32, (BF16_ROWS, vt.shape[1]), 0)
    return jnp.concatenate([vt, jnp.where(r == 0, 1.0, 0.0).astype(BF16)], axis=0)


def _modulated(x_ref, mod_ref):
    return x_ref[0] * (1.0 + mod_ref[0, 1:2, :]) + mod_ref[0, 0:1, :]


def _mod_kernel(c_ref, w_ref, b_ref, o_ref):
    c = c_ref[...]
    s = c / (1.0 + jnp.exp(-c))
    o_ref[0] = _dot(s.astype(BF16), w_ref[0].astype(BF16)) + b_ref[0]


def _modulation(cond, w_mod, b_mod):
    tn = 512
    n = 6 * D_MODEL
    out = pl.pallas_call(
        _mod_kernel,
        grid=(DEPTH, n // tn),
        in_specs=[pl.BlockSpec((MOD_ROWS, D_MODEL), lambda l, j: (0, 0)),
                  pl.BlockSpec((1, D_MODEL, tn), lambda l, j: (l, 0, j)),
                  pl.BlockSpec((1, 1, tn), lambda l, j: (l, 0, j))],
        out_specs=pl.BlockSpec((1, MOD_ROWS, tn), lambda l, j: (l, 0, j)),
        out_shape=jax.ShapeDtypeStruct((DEPTH, MOD_ROWS, n), F32),
        compiler_params=_cparams(("parallel", "parallel")),
        name="modulation",
    )(cond, w_mod, b_mod.reshape(DEPTH, 1, n))
    return out.reshape(DEPTH, MOD_ROWS, 6, D_MODEL)


def _proj_kernel(*refs, segs, tsegs, rope_n, cw):
    x_ref, mod_ref = refs[:2]
    pos = 2
    w_ref = wt_ref = None
    if segs:
        w_ref = refs[pos]
        pos += 1
    if tsegs:
        wt_ref = refs[pos]
        pos += 1
    if rope_n and any(s[2] for s in segs):
        cos, sup, sdn = (r[...] for r in refs[pos:pos + 3])
        pos += 3
    if rope_n and any(s[2] for s in tsegs):
        cos_t, sup_t, sdn_t = (r[...] for r in refs[pos:pos + 3])
        pos += 3
    outs = list(refs[pos:])
    h32 = _modulated(x_ref, mod_ref)
    h = h32.astype(BF16)
    for (start, width, rope, want16, want32, mul) in segs:
        o16 = outs.pop(0) if want16 else None
        o32 = outs.pop(0) if want32 else None
        for c in range(0, width, cw):
            acc = _dot(h, w_ref[:, start + c:start + c + cw])
            if want32:
                o32[0, :, c:c + cw] = acc
            if mul is not None:
                acc = acc * mul
            if rope:
                acc = jnp.concatenate(
                    [_rope_slab(acc[:, s:s + LANES], cos, sup, sdn, rope_n) for s in range(0, cw, LANES)], axis=1)
            if want16:
                o16[0, :, c:c + cw] = acc.astype(BF16)
    if tsegs:
        ht = h32.T.astype(BF16)
        for (start, width, rope, _, _, mul) in tsegs:
            o16 = outs.pop(0)
            for c in range(0, width, cw):
                acc = _dot(wt_ref[start + c:start + c + cw, :], ht)
                if mul is not None:
                    acc = acc * mul
                if rope:
                    acc = jnp.concatenate(
                        [_rope_slab_t(acc[s:s + LANES], cos_t, sup_t, sdn_t, rope_n) for s in range(0, cw, LANES)],
                        axis=0)
                o16[0, c:c + cw, :] = acc.astype(BF16)


def _project(x, mod, row_of, w, wt, segs, tsegs, rope=None, rope_t=None, tm=512, cw=256, name="proj"):
    B, T, _ = x.shape
    tm = min(tm, T)
    const = lambda b, i: (0, 0)
    in_specs = [pl.BlockSpec((1, tm, D_MODEL), lambda b, i: (b, i, 0)),
                pl.BlockSpec((1, 6, D_MODEL), lambda b, i: (row_of(b), 0, 0))]
    args = [x, mod]
    if segs:
        in_specs.append(pl.BlockSpec(w.shape, const))
        args.append(w)
    if tsegs:
        in_specs.append(pl.BlockSpec(wt.shape, const))
        args.append(wt)
    rope_n = 0
    if rope is not None:
        rope_n, tabs = rope
        in_specs += [pl.BlockSpec((tm, LANES), lambda b, i: (i, 0))] * 3
        args += list(tabs)
    if rope_t is not None:
        rope_n, tabs = rope_t
        in_specs += [pl.BlockSpec((LANES, tm), lambda b, i: (0, i))] * 3
        args += list(tabs)
    out_specs, out_shape = [], []
    for (_, width, _, want16, want32, _) in segs:
        for want, dt in ((want16, BF16), (want32, F32)):
            if want:
                out_specs.append(pl.BlockSpec((1, tm, width), lambda b, i: (b, i, 0)))
                out_shape.append(jax.ShapeDtypeStruct((B, T, width), dt))
    for (_, width, _, _, _, _) in tsegs:
        out_specs.append(pl.BlockSpec((1, width, tm), lambda b, i: (b, 0, i)))
        out_shape.append(jax.ShapeDtypeStruct((B, width, T), BF16))
    return pl.pallas_call(
        functools.partial(_proj_kernel, segs=tuple(segs), tsegs=tuple(tsegs), rope_n=rope_n, cw=cw),
        grid=(B, T // tm), in_specs=in_specs, out_specs=out_specs, out_shape=out_shape,
        compiler_params=_cparams(("parallel", "parallel")), name=name,
    )(*args)


def _mla_proj_kernel(*refs, lat, scale, cw):
    x_ref, mod_ref, wa_ref, qn_ref, kvn_ref, wuqt_ref, wk_ref, wvt_ref = refs[:8]
    pos = 8
    if lat:
        qcos, qsup, qsdn, kcos, ksup, ksdn = (r[...] for r in refs[8:14])
        pos = 14
    qt_out, k_out, vt_out = refs[pos:pos + 3]
    h = _modulated(x_ref, mod_ref).astype(BF16)
    a = _dot(h, wa_ref[...])
    cq = _rms_norm(a[:, :MLA_Q_RANK], qn_ref[...])
    ckv = _rms_norm(a[:, MLA_Q_RANK:MLA_Q_RANK + MLA_KV_RANK], kvn_ref[...])
    kpe = a[:, MLA_Q_RANK + MLA_KV_RANK:]
    if lat:
        kpe = _rope_slab(kpe, kcos, ksup, ksdn, MLA_ROPE // 4)
    else:
        ckv_state, kpe_state = refs[pos + 3:pos + 5]
        ckv_state[0] = ckv
        kpe_state[0] = kpe[:, :MLA_ROPE]
    cqt = cq.T.astype(BF16)
    n_q = wuqt_ref.shape[0]
    for c in range(0, n_q, cw):
        q = _dot(wuqt_ref[c:c + cw, :], cqt) * scale
        if lat:
            q = jnp.concatenate(
                [_rope_slab_t(q[s:s + LANES], qcos, qsup, qsdn, MLA_ROPE // 4) for s in range(0, cw, LANES)], axis=0)
        qt_out[0, c:c + cw, :] = q.astype(BF16)
    kin = jnp.concatenate([ckv.astype(BF16), kpe.astype(BF16)], axis=1)
    for c in range(0, wk_ref.shape[1], cw):
        k_out[0, :, c:c + cw] = _dot(kin, wk_ref[:, c:c + cw]).astype(BF16)
    ckvt = ckv.T.astype(BF16)
    for c in range(0, wvt_ref.shape[0], cw):
        vt_out[0, c:c + cw, :] = _dot(wvt_ref[c:c + cw, :], ckvt).astype(BF16)


def _mla_project(x, mod, row_of, wa, qn, kvn, wuqt, wk, wvt, scale, rope=None, tm=512, name="mla_proj"):
    B, T, _ = x.shape
    tm = min(tm, T)
    lat = rope is not None
    const = lambda b, i: (0, 0)
    in_specs = [pl.BlockSpec((1, tm, D_MODEL), lambda b, i: (b, i, 0)),
                pl.BlockSpec((1, 6, D_MODEL), lambda b, i: (row_of(b), 0, 0))]
    in_specs += [pl.BlockSpec(a.shape, const) for a in (wa, qn, kvn, wuqt, wk, wvt)]
    args = [x, mod, wa, qn, kvn, wuqt, wk, wvt]
    if lat:
        in_specs += [pl.BlockSpec((LANES, tm), lambda b, i: (0, i))] * 3
        in_specs += [pl.BlockSpec((tm, LANES), lambda b, i: (i, 0))] * 3
        args += list(rope)
    tok = lambda b, i: (b, i, 0)
    feat = lambda b, i: (b, 0, i)
    out_specs = [pl.BlockSpec((1, wuqt.shape[0], tm), feat), pl.BlockSpec((1, tm, wk.shape[1]), tok),
                 pl.BlockSpec((1, wvt.shape[0], tm), feat)]
    out_shape = [jax.ShapeDtypeStruct((B, wuqt.shape[0], T), BF16), jax.ShapeDtypeStruct((B, T, wk.shape[1]), BF16),
                 jax.ShapeDtypeStruct((B, wvt.shape[0], T), BF16)]
    if not lat:
        out_specs += [pl.BlockSpec((1, tm, MLA_KV_RANK), tok), pl.BlockSpec((1, tm, MLA_ROPE), tok)]
        out_shape += [jax.ShapeDtypeStruct((B, T, MLA_KV_RANK), F32), jax.ShapeDtypeStruct((B, T, MLA_ROPE), F32)]
    return pl.pallas_call(
        functools.partial(_mla_proj_kernel, lat=lat, scale=scale, cw=512),
        grid=(B, T // tm), in_specs=in_specs, out_specs=out_specs, out_shape=out_shape,
        compiler_params=_cparams(("parallel", "parallel")), name=name,
    )(*args)


def _mla_cache_kernel(ckv_ref, kpe_ref, wk_ref, wvt_ref, k_out, vt_out):
    ckv = ckv_ref[0]
    kin = jnp.concatenate([ckv.astype(BF16), kpe_ref[0].astype(BF16)], axis=1)
    k_out[0] = _dot(kin, wk_ref[...]).astype(BF16)
    vt_out[0] = _dot(wvt_ref[...], ckv.T.astype(BF16)).astype(BF16)


def _mla_cache_expand(c_ckv, c_kpe_slab, wk, wvt):
    B, P, _ = c_ckv.shape
    const = lambda b: (0, 0)
    return pl.pallas_call(
        _mla_cache_kernel, grid=(B,),
        in_specs=[pl.BlockSpec((1, P, MLA_KV_RANK), lambda b: (b, 0, 0)),
                  pl.BlockSpec((1, P, LANES), lambda b: (b, 0, 0)),
                  pl.BlockSpec(wk.shape, const), pl.BlockSpec(wvt.shape, const)],
        out_specs=[pl.BlockSpec((1, P, wk.shape[1]), lambda b: (b, 0, 0)),
                   pl.BlockSpec((1, wvt.shape[0], P), lambda b: (b, 0, 0))],
        out_shape=[jax.ShapeDtypeStruct((B, P, wk.shape[1]), BF16),
                   jax.ShapeDtypeStruct((B, wvt.shape[0], P), BF16)],
        compiler_params=_cparams(("parallel",)), name="mla_cache_expand",
    )(c_ckv, c_kpe_slab, wk, wvt)


def _flash_kernel(*refs, mode, has_cache, gpb, tq, tk):
    qt_ref = refs[0]
    pos = 1
    if has_cache:
        kc_ref, vct_ref = refs[1:3]
        pos = 3
    k_ref, vt_ref = refs[pos:pos + 2]
    pos += 2
    if mode == "diff":
        lam_ref, subln_ref = refs[pos:pos + 2]
        pos += 2
    o_ref, s_sc, acc_sc = refs[pos:pos + 3]
    dv = LANES if mode == "diff" else HALF
    wq = LANES if mode == "diff" else 2 * LANES

    blocks = []
    if has_cache:
        blocks += [(kc_ref, vct_ref, j) for j in range(kc_ref.shape[1] // tk)]
    blocks += [(k_ref, vt_ref, j) for j in range(k_ref.shape[1] // tk)]

    qts = [qt_ref[0, g * wq:(g + 1) * wq, :] for g in range(gpb)]
    if mode == "diff":
        qts = [_split_heads_t(qt) for qt in qts]

    def logits(n, g):
        kr, _, j = blocks[n]
        k = kr[0, j * tk:(j + 1) * tk, g * wq:(g + 1) * wq]
        buf = 2 * g + n % 2
        if mode == "diff":
            s_sc[buf] = _dot(k, qts[g])
        else:
            s_sc[buf, :, :tq] = _dot(k[:, :LANES], qts[g][:LANES])
            s_sc[buf, :, tq:] = _dot(k[:, LANES:], qts[g][LANES:])

    for g in range(gpb):
        logits(0, g)
    m = [None] * gpb
    for n in range(len(blocks)):
        for g in range(gpb):
            if n + 1 < len(blocks):
                logits(n + 1, g)
            _, vr, j = blocks[n]
            vt = vr[0, g * LANES:(g + 1) * LANES, j * tk:(j + 1) * tk]
            s = s_sc[2 * g + n % 2]
            m_cur = jnp.max(s, 0, keepdims=True)
            m_new = m_cur if m[g] is None else jnp.maximum(m[g], m_cur)
            pb = jnp.exp2(s - m_new).astype(BF16)
            if mode == "diff":
                pv = _dot(_with_ones_row(vt), pb)
            else:
                pv = jnp.concatenate([_dot(_with_ones_row(vt[:HALF]), pb[:, :tq]),
                                      _dot(_with_ones_row(vt[HALF:]), pb[:, tq:])], axis=1)
            if m[g] is None:
                acc_sc[g] = pv
            else:
                acc_sc[g] = jnp.exp2(m[g] - m_new) * acc_sc[g] + pv
            m[g] = m_new

    if mode == "diff":
        lam = lam_ref[...]
        lam_full = (jnp.exp(jnp.sum(lam[0:1] * lam[1:2], keepdims=True))
                    - jnp.exp(jnp.sum(lam[2:3] * lam[3:4], keepdims=True)) + DIFF_LAMBDA_INIT)
    for g in range(gpb):
        acc = acc_sc[g]
        o = acc[:dv] / acc[dv:dv + 1]
        o0, o1 = o[:, :tq], o[:, tq:]
        if mode == "diff":
            out_t = _rms_norm(o0 - lam_full * o1, subln_ref[...], axis=0) * (1.0 - DIFF_LAMBDA_INIT)
        else:
            out_t = jnp.concatenate([o0, o1], axis=0)
        o_ref[0, :, g * LANES:(g + 1) * LANES] = out_t.T.astype(o_ref.dtype)


def _flash(qt, k, vt, cache=None, *, mode, extra=(), gpb=1, tq=1024, tk=512, name="flash"):
    B, _, T = qt.shape
    S = k.shape[1]
    wq = LANES if mode == "diff" else 2 * LANES
    G = vt.shape[1] // LANES
    tq, tk = min(tq, T), min(tk, S)
    in_specs = [pl.BlockSpec((1, gpb * wq, tq), lambda b, g, i: (b, g, i))]
    args = [qt]
    if cache is not None:
        kc, vct = cache
        P = kc.shape[1]
        in_specs += [pl.BlockSpec((1, P, gpb * wq), lambda b, g, i: (b, 0, g)),
                     pl.BlockSpec((1, gpb * LANES, P), lambda b, g, i: (b, g, 0))]
        args += [kc, vct]
    in_specs += [pl.BlockSpec((1, S, gpb * wq), lambda b, g, i: (b, 0, g)),
                 pl.BlockSpec((1, gpb * LANES, S), lambda b, g, i: (b, g, 0))]
    args += [k, vt]
    for e in extra:
        in_specs.append(pl.BlockSpec(e.shape, lambda b, g, i: (0, 0)))
        args.append(e)
    dv = LANES if mode == "diff" else HALF
    return pl.pallas_call(
        functools.partial(_flash_kernel, mode=mode, has_cache=cache is not None, gpb=gpb, tq=tq, tk=tk),
        grid=(B, G // gpb, T // tq), in_specs=in_specs,
        out_specs=pl.BlockSpec((1, tq, gpb * LANES), lambda b, g, i: (b, i, g)),
        out_shape=jax.ShapeDtypeStruct((B, T, G * LANES), BF16),
        scratch_shapes=[pltpu.VMEM((2 * gpb, tk, 2 * tq), F32), pltpu.VMEM((gpb, dv + BF16_ROWS, 2 * tq), F32)],
        compiler_params=_cparams(("parallel", "parallel", "parallel")), name=name,
    )(*args)


def _band_kernel(*refs, n_seg, seg_bias, n_slab, gpb, has_bias, bias_groups, has_sink, tq):
    qt_ref = refs[0]
    kv = refs[1:1 + 2 * n_seg]
    pos = 1 + 2 * n_seg
    bias_ref = sink_ref = None
    if has_bias:
        bias_ref = refs[pos]
        pos += 1
    if has_sink:
        sink_ref = refs[pos]
        pos += 1
    o_ref, s_sc = refs[pos:pos + 2]
    rows = [kv[2 * i].shape[1] for i in range(n_seg)]
    offs = [sum(rows[:i]) for i in range(n_seg)]

    def logits(g):
        q0 = g * n_slab * LANES
        qst = jnp.concatenate(
            [_split_heads_t(qt_ref[0, q0 + s * LANES:q0 + (s + 1) * LANES, :]) for s in range(n_slab)], axis=1)
        for i in range(n_seg):
            lg = _dot(kv[2 * i][0, :, g * LANES:(g + 1) * LANES], qst)
            if seg_bias[i] is not None:
                lg = lg + bias_ref[0, g if bias_groups > 1 else 0, seg_bias[i]:seg_bias[i] + rows[i], :]
            s_sc[g % 2, offs[i]:offs[i] + rows[i], :] = lg

    logits(0)
    for g in range(gpb):
        if g + 1 < gpb:
            logits(g + 1)
        q0 = g * n_slab * LANES
        s = s_sc[g % 2]
        m = jnp.max(s, 0, keepdims=True)
        if has_sink:
            sink = sink_ref[g]
            m = jnp.maximum(m, sink)
        vt = jnp.concatenate([kv[2 * i + 1][0, g * LANES:(g + 1) * LANES, :] for i in range(n_seg)], axis=1)
        acc = _dot(_with_ones_row(vt), jnp.exp2(s - m).astype(BF16))
        l = acc[LANES:LANES + 1]
        if has_sink:
            l = l + jnp.exp2(sink - m)
        o = acc[:LANES] / l
        for s in range(n_slab):
            c = 2 * s * tq
            out_t = jnp.concatenate([o[:HALF, c:c + tq], o[HALF:, c + tq:c + 2 * tq]], axis=0)
            o_ref[0, :, q0 + s * LANES:q0 + (s + 1) * LANES] = out_t.T.astype(o_ref.dtype)


def _band(qt, segs, *, n_slab, tq, gpb=1, bias=None, sink=None, name="band"):
    B, _, T = qt.shape
    G = qt.shape[1] // (n_slab * LANES)
    nq = T // tq
    R = n_slab * 2 * tq
    in_specs = [pl.BlockSpec((1, gpb * n_slab * LANES, tq), lambda g, b, i: (b, g, i))]
    args = [qt]
    seg_bias = []
    for (k, vt, ts, blk_of, boff) in segs:
        in_specs += [pl.BlockSpec((1, ts, gpb * LANES), (lambda f: lambda g, b, i: (b, f(i, nq), g))(blk_of)),
                     pl.BlockSpec((1, gpb * LANES, ts), (lambda f: lambda g, b, i: (b, g, f(i, nq)))(blk_of))]
        args += [k, vt]
        seg_bias.append(boff)
    bias_groups = 1
    if bias is not None:
        n_var, bias_groups, sb, _ = bias.shape
        var_of = lambda i: jnp.where(i == 0, 0, jnp.where(i == nq - 1, n_var - 1, 1))
        nb = gpb if bias_groups > 1 else 1
        in_specs.append(pl.BlockSpec((1, nb, sb, R), lambda g, b, i: (var_of(i), g if bias_groups > 1 else 0, 0, 0)))
        args.append(bias)
    if sink is not None:
        in_specs.append(pl.BlockSpec((gpb, 1, R), lambda g, b, i: (g, 0, 0)))
        args.append(sink)
    return pl.pallas_call(
        functools.partial(_band_kernel, n_seg=len(segs), seg_bias=tuple(seg_bias), n_slab=n_slab, gpb=gpb,
                          has_bias=bias is not None, bias_groups=bias_groups, has_sink=sink is not None, tq=tq),
        grid=(G // gpb, B, nq), in_specs=in_specs,
        out_specs=pl.BlockSpec((1, tq, gpb * n_slab * LANES), lambda g, b, i: (b, i, g)),
        out_shape=jax.ShapeDtypeStruct((B, T, G * n_slab * LANES), BF16),
        scratch_shapes=[pltpu.VMEM((2, sum(s[2] for s in segs), R), F32)],
        compiler_params=_cparams(("parallel", "parallel", "parallel")), name=name,
    )(*args)


_OWN = lambda i, nq: i
_PREV = lambda i, nq: jnp.maximum(i - 1, 0)
_NEXT = lambda i, nq: jnp.minimum(i + 1, nq - 1)
_FIRST = lambda i, nq: 0


def _post_kernel(x_ref, o_ref, mod_ref, g_ref, b_ref, wo_ref, w1_ref, w2_ref, out_ref, *, cf):
    x = x_ref[0]
    y = _dot(o_ref[0], wo_ref[...])
    x1 = _layer_norm(ALPHA * x + mod_ref[0, 2:3, :] * y, g_ref[0:1, :], b_ref[0:1, :])
    h = (x1 * (1.0 + mod_ref[0, 4:5, :]) + mod_ref[0, 3:4, :]).astype(BF16)
    acc = jnp.zeros(x.shape, F32)
    for c in range(0, D_FF, cf):
        u = jnp.maximum(_dot(h, w1_ref[:, c:c + cf]), 0.0)
        acc = acc + _dot((u * u).astype(BF16), w2_ref[c:c + cf, :])
    out_ref[0] = _layer_norm(ALPHA * x1 + mod_ref[0, 5:6, :] * acc, g_ref[1:2, :], b_ref[1:2, :])


def _post(x, o, mod, row_of, ln_g, ln_b, wo, w1, w2, tm=512, name="post"):
    B, T, _ = x.shape
    tm = min(tm, T)
    const = lambda b, i: (0, 0)
    once = pl.Buffered(1)
    return pl.pallas_call(
        functools.partial(_post_kernel, cf=1024),
        grid=(B, T // tm),
        in_specs=[pl.BlockSpec((1, tm, D_MODEL), lambda b, i: (b, i, 0)),
                  pl.BlockSpec((1, tm, D_MODEL), lambda b, i: (b, i, 0)),
                  pl.BlockSpec((1, 6, D_MODEL), lambda b, i: (row_of(b), 0, 0)),
                  pl.BlockSpec((2, D_MODEL), const), pl.BlockSpec((2, D_MODEL), const),
                  pl.BlockSpec(wo.shape, const, pipeline_mode=once),
                  pl.BlockSpec(w1.shape, const, pipeline_mode=once),
                  pl.BlockSpec(w2.shape, const, pipeline_mode=once)],
        out_specs=pl.BlockSpec((1, tm, D_MODEL), lambda b, i: (b, i, 0)),
        out_shape=jax.ShapeDtypeStruct(x.shape, F32),
        compiler_params=_cparams(("parallel", "parallel")), name=name,
    )(x, o, mod, ln_g, ln_b, wo, w1, w2)


def _rope_tables(T, R, lead, reps):
    n = R // 4
    t = jnp.arange(T)
    inv = ROPE_BASE ** (-jnp.arange(n, dtype=jnp.float32) / n)

    def cs(p):
        ang = p.astype(jnp.float32)[:, None] * inv
        return jnp.cos(ang), jnp.sin(ang)

    cr, sr = cs(t // GRID_W)
    cc, sc = cs(t % GRID_W)
    z = jnp.zeros_like(sr)
    cos = jnp.concatenate([cr, cr, cc, cc], -1)
    sup = jnp.concatenate([-sr, z, -sc, z], -1)
    sdn = jnp.concatenate([z, sr, z, sc], -1)
    tail = LANES - lead - reps * R

    def slab(a, fill):
        parts = [jnp.full((T, lead), fill, F32)] if lead else []
        parts += [a] * reps
        if tail:
            parts.append(jnp.full((T, tail), fill, F32))
        return jnp.concatenate(parts, -1)

    return slab(cos, 1.0), slab(sup, 0.0), slab(sdn, 0.0)


def _na_bias_tables(rpb, rows):
    qr, kb_n = NA_QROWS, 3 * NA_QROWS
    n_dr, n_dc = 2 * NA_KH - 1, 2 * NA_KW - 1
    c = np.arange(GRID_W)
    c0 = np.clip(c - NA_KW // 2, 0, GRID_W - NA_KW)
    kc = np.arange(GRID_W)[:, None]
    col_ok = (kc >= c0[None]) & (kc < c0[None] + NA_KW)
    sel_c = ((np.arange(n_dc)[:, None, None] == (kc - c[None] + NA_KW - 1)[None]) & col_ok[None])
    sel_r = np.zeros((3, n_dr, kb_n, qr), bool)
    for v, r_base in enumerate((0, qr, rows - qr)):
        r = r_base + np.arange(qr)[None]
        kr = r_base - qr + np.arange(kb_n)[:, None]
        r0 = np.clip(r - NA_KH // 2, 0, rows - NA_KH)
        row_ok = (kr >= r0) & (kr < r0 + NA_KH)
        sel_r[v] = (np.arange(n_dr)[:, None, None] == (kr - r + NA_KH - 1)[None]) & row_ok[None]
    hp = lax.Precision.HIGHEST
    H = rpb.shape[0]
    rpb2 = (rpb.astype(F32) * LOG2E).reshape(H // 2, 2, n_dr, n_dc)
    t1 = jnp.einsum("gxde,ejc->gxdjc", rpb2, jnp.asarray(sel_c, F32), precision=hp)
    tab = jnp.einsum("vdkr,gxdjc->vgkjxrc", jnp.asarray(sel_r, F32), t1, precision=hp)
    ok = np.einsum("vdkr,ejc->vkjrc", sel_r.astype(np.int32), sel_c.astype(np.int32)) > 0
    off = np.where(ok, 0.0, NEG_INF).astype(np.float32)[:, None, :, :, None]
    return (tab + jnp.asarray(off)).reshape(3, H // 2, kb_n * GRID_W, 2 * qr * GRID_W)


def _swa_mask_tables(nb, reps):
    qi = np.arange(SWA_QB)[None, :]
    kj = np.arange(3 * SWA_QB)[:, None]
    tabs = []
    for blk in (0, 1, nb - 1):
        kpos = (blk - 1) * SWA_QB + kj
        ok = (np.abs(kj - SWA_QB - qi) <= SWA_WINDOW) & (kpos >= 0) & (kpos < nb * SWA_QB)
        tabs.append(np.tile(np.where(ok, 0.0, NEG_INF).astype(np.float32), (1, reps))[None])
    return jnp.asarray(np.stack(tabs, 0))


_SWA_PERM = np.array([8 * p + 4 * half + s for p in range(2) for s in range(4) for half in range(2)])


def _perm_heads(w, axis, dh=HALF):
    shape = w.shape
    w = w.reshape(shape[:axis] + (SWA_HEADS, dh) + shape[axis + 1:])
    w = jnp.take(w, jnp.asarray(_SWA_PERM), axis=axis)
    return w.reshape(shape)


def kernel(x_prompt, x_sample, cache_l0_k, cache_l0_v, cache_l1_k, cache_l1_v, cache_l2_ckv, cache_l2_kpe, cache_l3_k, cache_l3_v, c, c_ctx, w_mod, b_mod, ln_g, ln_b, w_mlp1, w_mlp2, l0_w_qkv, l0_lam, l0_subln, l0_w_o, l1_w_qkv, l1_rpb, l1_w_o, l2_w_a, l2_q_norm, l2_kv_norm, l2_w_uq, l2_w_ukv, l2_w_o, l3_w_qkv, l3_sink, l3_w_o):
    D = D_MODEL
    Bc, S, _ = x_prompt.shape
    Bl, T, _ = x_sample.shape
    P = cache_l0_k.shape[1]
    assert 1 + Bl <= MOD_ROWS and T % GRID_W == 0
    rows = T // GRID_W

    cond = jnp.concatenate([c_ctx[None], c, jnp.zeros((MOD_ROWS - 1 - Bl, D), F32)], 0)
    mod = _modulation(cond, w_mod, b_mod)
    ctx_row = lambda b: 0
    lat_row = lambda b: b + 1

    bf = lambda a: a.astype(BF16)
    tr = lambda a: jnp.swapaxes(a, -1, -2)
    w1 = bf(w_mlp1)
    w2 = bf(w_mlp2)
    qmul = HALF ** -0.5 * LOG2E

    tabs64 = _rope_tables(T, DIFF_DH, 0, 2)
    rope64 = (DIFF_DH // 4, tabs64)
    rope64_t = (DIFF_DH // 4, tuple(tr(t) for t in tabs64))

    def qkv_t(x, mod_l, row_of, w_k, w_t, nq, nk, lat, rope, name):
        if lat:
            return _project(x, mod_l, row_of, w_k, w_t, [(0, nk, rope, True, False, None)],
                            [(0, nq, rope, True, False, qmul), (nq, nk, False, True, False, None)],
                            rope=rope64 if rope else None, rope_t=rope64_t if rope else None, name=name)
        return _project(x, mod_l, row_of, w_k, w_t,
                        [(0, nk, False, True, True, None), (nk, nk, False, False, True, None)],
                        [(0, nq, False, True, False, qmul), (nq, nk, False, True, False, None)], name=name)

    def split_w(w, nq, nk):
        return bf(w[:, nq:]), bf(tr(jnp.concatenate([w[:, :nq], w[:, nq + nk:]], 1)))

    w_k, w_t = split_w(l0_w_qkv, D, D)
    wo = bf(l0_w_o)
    lam = l0_lam.astype(F32)
    subln = l0_subln.reshape(2 * DIFF_DH, 1)
    k, k0, v0, qt, vt = qkv_t(x_prompt, mod[0], ctx_row, w_k, w_t, D, D, False, False, "l0_proj_ctx")
    o = _flash(qt, k, vt, mode="diff", extra=(lam, subln), gpb=DIFF_HEADS, name="l0_attn_ctx")
    xp = _post(x_prompt, o, mod[0], ctx_row, ln_g[0], ln_b[0], wo, w1[0], w2[0], name="l0_post_ctx")
    k, qt, vt = qkv_t(x_sample, mod[0], lat_row, w_k, w_t, D, D, True, True, "l0_proj_lat")
    cache = (bf(cache_l0_k.reshape(Bl, P, D)), bf(tr(cache_l0_v.reshape(Bl, P, D))))
    o = _flash(qt, k, vt, cache, mode="diff", extra=(lam, subln), gpb=2, tq=512, name="l0_attn_lat")
    xs = _post(x_sample, o, mod[0], lat_row, ln_g[0], ln_b[0], wo, w1[0], w2[0], name="l0_post_lat")

    w_k, w_t = split_w(l1_w_qkv, D, D)
    wo = bf(l1_w_o)
    k, k1, v1, qt, vt = qkv_t(xp, mod[1], ctx_row, w_k, w_t, D, D, False, False, "l1_proj_ctx")
    o = _band(qt, [(k, vt, S, _OWN, None)], n_slab=1, tq=S, gpb=NA_HEADS // 2, name="l1_attn_ctx")
    xp = _post(xp, o, mod[1], ctx_row, ln_g[1], ln_b[1], wo, w1[1], w2[1], name="l1_post_ctx")
    k, qt, vt = qkv_t(xs, mod[1], lat_row, w_k, w_t, D, D, True, False, "l1_proj_lat")
    kc, vct = bf(cache_l1_k.reshape(Bl, P, D)), bf(tr(cache_l1_v.reshape(Bl, P, D)))
    tqn = NA_QROWS * GRID_W
    bias = _na_bias_tables(l1_rpb, rows)
    o = _band(qt, [(k, vt, tqn, _PREV, 0), (k, vt, tqn, _OWN, tqn), (k, vt, tqn, _NEXT, 2 * tqn),
                   (kc, vct, P, _FIRST, None)], n_slab=1, tq=tqn, gpb=NA_HEADS // 2, bias=bias, name="l1_attn_lat")
    xs = _post(xs, o, mod[1], lat_row, ln_g[1], ln_b[1], wo, w1[1], w2[1], name="l1_post_lat")

    H, dq = MLA_HEADS, MLA_NOPE + MLA_ROPE
    wa = bf(jnp.pad(l2_w_a, ((0, 0), (0, 7 * LANES - l2_w_a.shape[1]))))
    wuq = jnp.pad(l2_w_uq.reshape(MLA_Q_RANK, H, dq), ((0, 0), (0, 0), (0, LANES - dq))).reshape(MLA_Q_RANK, H * LANES)
    wuqt = bf(tr(wuq))
    wukv = l2_w_ukv.reshape(MLA_KV_RANK, H, MLA_NOPE + MLA_VDIM)
    wk_n = jnp.pad(wukv[:, :, :MLA_NOPE], ((0, 0), (0, 0), (0, LANES - MLA_NOPE))).reshape(MLA_KV_RANK, H * LANES)
    place = np.zeros((LANES, H, LANES), np.float32)
    place[np.arange(MLA_ROPE), :, MLA_NOPE + np.arange(MLA_ROPE)] = 1.0
    wk = bf(jnp.concatenate([wk_n, jnp.asarray(place.reshape(LANES, H * LANES))], 0))
    wvt = bf(tr(wukv[:, :, MLA_NOPE:].reshape(MLA_KV_RANK, H * MLA_VDIM)))
    wo = bf(l2_w_o)
    qn, kvn = l2_q_norm.reshape(1, -1), l2_kv_norm.reshape(1, -1)
    mscale = dq ** -0.5 * LOG2E
    qt, k, vt, ckv2, kpe2 = _mla_project(xp, mod[2], ctx_row, wa, qn, kvn, wuqt, wk, wvt, mscale, name="l2_proj_ctx")
    o = _flash(qt, k, vt, mode="mla", gpb=MLA_HEADS // 2, name="l2_attn_ctx")
    xp = _post(xp, o, mod[2], ctx_row, ln_g[2], ln_b[2], wo, w1[2], w2[2], name="l2_post_ctx")
    tabs = tuple(tr(t) for t in _rope_tables(T, MLA_ROPE, MLA_NOPE, 1)) + _rope_tables(T, MLA_ROPE, 0, 1)
    qt, k, vt = _mla_project(xs, mod[2], lat_row, wa, qn, kvn, wuqt, wk, wvt, mscale, rope=tabs, name="l2_proj_lat")
    kpe_slab = jnp.pad(cache_l2_kpe, ((0, 0), (0, 0), (0, LANES - MLA_ROPE)))
    cache = _mla_cache_expand(cache_l2_ckv, kpe_slab, wk, wvt)
    o = _flash(qt, k, vt, cache, mode="mla", name="l2_attn_lat")
    xs = _post(xs, o, mod[2], lat_row, ln_g[2], ln_b[2], wo, w1[2], w2[2], name="l2_post_lat")

    nq3, nk3 = SWA_HEADS * HALF, SWA_KV_HEADS * HALF
    w = jnp.concatenate([_perm_heads(l3_w_qkv[:, :nq3], 1), l3_w_qkv[:, nq3:]], 1)
    w_k, w_t = split_w(w, nq3, nk3)
    wo = bf(_perm_heads(l3_w_o, 0))
    sink = jnp.take(l3_sink.astype(F32), jnp.asarray(_SWA_PERM)) * LOG2E

    def sink_rows(tq):
        return jnp.repeat(sink.reshape(2, 1, 8), tq, axis=-1)

    k, k3, v3, qt, vt = qkv_t(xp, mod[3], ctx_row, w_k, w_t, nq3, nk3, False, False, "l3_proj_ctx")
    o = _band(qt, [(k, vt, S, _OWN, None)], n_slab=4, tq=S, gpb=2, sink=sink_rows(S), name="l3_attn_ctx")
    xp = _post(xp, o, mod[3], ctx_row, ln_g[3], ln_b[3], wo, w1[3], w2[3], name="l3_post_ctx")
    k, qt, vt = qkv_t(xs, mod[3], lat_row, w_k, w_t, nq3, nk3, True, True, "l3_proj_lat")
    kc, vct = bf(cache_l3_k.reshape(Bl, P, nk3)), bf(tr(cache_l3_v.reshape(Bl, P, nk3)))
    mask = _swa_mask_tables(T // SWA_QB, 8)
    o = _band(qt, [(k, vt, SWA_QB, _PREV, 0), (k, vt, SWA_QB, _OWN, SWA_QB), (k, vt, SWA_QB, _NEXT, 2 * SWA_QB),
                   (kc, vct, P, _FIRST, None)], n_slab=4, tq=SWA_QB, gpb=2, bias=mask, sink=sink_rows(SWA_QB),
              name="l3_attn_lat")
    xs = _post(xs, o, mod[3], lat_row, ln_g[3], ln_b[3], wo, w1[3], w2[3], name="l3_post_lat")

    return (xp, xs,
            k0.reshape(Bc, S, DIFF_HEADS, 2 * DIFF_DH), v0.reshape(Bc, S, DIFF_HEADS, 2 * DIFF_DH),
            k1.reshape(Bc, S, NA_HEADS, HALF), v1.reshape(Bc, S, NA_HEADS, HALF),
            ckv2, kpe2,
            k3.reshape(Bc, S, SWA_KV_HEADS, HALF), v3.reshape(Bc, S, SWA_KV_HEADS, HALF))
```

```python
import functools
import math

import numpy as np
import jax
import jax.numpy as jnp
from jax import lax
from jax.experimental import pallas as pl
from jax.experimental.pallas import tpu as pltpu

D_MODEL = 1024
DEPTH = 4
GRID_W = 64
ROPE_BASE = 10000.0
LN_EPS = 1e-5
NEG_INF = -1e30
ALPHA = (2 * DEPTH) ** 0.25
D_FF = 4 * D_MODEL
DIFF_DH = 64
DIFF_HEADS = 8
DIFF_LAMBDA_INIT = 0.8 - 0.6 * math.exp(-0.3 * 0)
NA_HEADS = 16
NA_KH = 8
NA_KW = 16
NA_QROWS = 4
MLA_HEADS = 16
MLA_NOPE = 64
MLA_ROPE = 32
MLA_VDIM = 64
MLA_Q_RANK = 512
MLA_KV_RANK = 256
SWA_HEADS = 16
SWA_KV_HEADS = 4
SWA_WINDOW = 128
SWA_QB = 128
LOG2E = math.log2(math.e)

LANES = 128
HALF = LANES // 2
BF16_ROWS = 16
MOD_ROWS = 16
VMEM_LIMIT = 56 * 1024 * 1024

F32 = jnp.float32
BF16 = jnp.bfloat16


def _cparams(sem):
    return pltpu.CompilerParams(dimension_semantics=sem, vmem_limit_bytes=VMEM_LIMIT)


def _dot(a, b):
    return jnp.dot(a, b, preferred_element_type=F32)


def _layer_norm(x, g, b):
    mu = jnp.mean(x, -1, keepdims=True)
    xc = x - mu
    var = jnp.mean(xc * xc, -1, keepdims=True)
    return xc * lax.rsqrt(var + LN_EPS) * g + b


def _rms_norm(x, g, axis=-1):
    return x * lax.rsqrt(jnp.mean(x * x, axis, keepdims=True) + LN_EPS) * g


def _rope_slab(x, cos, sin_up, sin_dn, n):
    return x * cos + pltpu.roll(x, LANES - n, 1) * sin_up + pltpu.roll(x, n, 1) * sin_dn


def _rope_slab_t(x, cos, sin_up, sin_dn, n):
    up = jnp.concatenate([x[n:], x[:n]], axis=0)
    dn = jnp.concatenate([x[LANES - n:], x[:LANES - n]], axis=0)
    return x * cos + up * sin_up + dn * sin_dn


def _split_heads_t(qt):
    lo = lax.broadcasted_iota(jnp.int32, qt.shape, 0) < HALF
    zero = jnp.zeros_like(qt)
    return jnp.concatenate([jnp.where(lo, qt, zero), jnp.where(lo, zero, qt)], axis=1)


def _with_ones_row(vt):
    r = lax.broadcasted_iota(jnp.int32, (BF16_ROWS, vt.shape[1]), 0)
    return jnp.concatenate([vt, jnp.where(r == 0, 1.0, 0.0).astype(BF16)], axis=0)


def _modulated(x_ref, mod_ref):
    return x_ref[0] * (1.0 + mod_ref[0, 1:2, :]) + mod_ref[0, 0:1, :]


def _mod_kernel(c_ref, w_ref, b_ref, o_ref):
    c = c_ref[...]
    s = c / (1.0 + jnp.exp(-c))
    o_ref[0] = _dot(s.astype(BF16), w_ref[0].astype(BF16)) + b_ref[0]


def _modulation(cond, w_mod, b_mod):
    tn = 512
    n = 6 * D_MODEL
    out = pl.pallas_call(
        _mod_kernel,
        grid=(DEPTH, n // tn),
        in_specs=[pl.BlockSpec((MOD_ROWS, D_MODEL), lambda l, j: (0, 0)),
                  pl.BlockSpec((1, D_MODEL, tn), lambda l, j: (l, 0, j)),
                  pl.BlockSpec((1, 1, tn), lambda l, j: (l, 0, j))],
        out_specs=pl.BlockSpec((1, MOD_ROWS, tn), lambda l, j: (l, 0, j)),
        out_shape=jax.ShapeDtypeStruct((DEPTH, MOD_ROWS, n), F32),
        compiler_params=_cparams(("parallel", "parallel")),
        name="modulation",
    )(cond, w_mod, b_mod.reshape(DEPTH, 1, n))
    return out.reshape(DEPTH, MOD_ROWS, 6, D_MODEL)


def _proj_kernel(*refs, segs, tsegs, rope_n, cw):
    x_ref, mod_ref = refs[:2]
    pos = 2
    w_ref = wt_ref = None
    if segs:
        w_ref = refs[pos]
        pos += 1
    if tsegs:
        wt_ref = refs[pos]
        pos += 1
    if rope_n and any(s[2] for s in segs):
        cos, sup, sdn = (r[...] for r in refs[pos:pos + 3])
        pos += 3
    if rope_n and any(s[2] for s in tsegs):
        cos_t, sup_t, sdn_t = (r[...] for r in refs[pos:pos + 3])
        pos += 3
    outs = list(refs[pos:])
    h32 = _modulated(x_ref, mod_ref)
    h = h32.astype(BF16)
    for (start, width, rope, want16, want32, mul) in segs:
        o16 = outs.pop(0) if want16 else None
        o32 = outs.pop(0) if want32 else None
        for c in range(0, width, cw):
            acc = _dot(h, w_ref[:, start + c:start + c + cw])
            if want32:
                o32[0, :, c:c + cw] = acc
            if mul is not None:
                acc = acc * mul
            if rope:
                acc = jnp.concatenate(
                    [_rope_slab(acc[:, s:s + LANES], cos, sup, sdn, rope_n) for s in range(0, cw, LANES)], axis=1)
            if want16:
                o16[0, :, c:c + cw] = acc.astype(BF16)
    if tsegs:
        ht = h32.T.astype(BF16)
        for (start, width, rope, _, _, mul) in tsegs:
            o16 = outs.pop(0)
            for c in range(0, width, cw):
                acc = _dot(wt_ref[start + c:start + c + cw, :], ht)
                if mul is not None:
                    acc = acc * mul
                if rope:
                    acc = jnp.concatenate(
                        [_rope_slab_t(acc[s:s + LANES], cos_t, sup_t, sdn_t, rope_n) for s in range(0, cw, LANES)],
                        axis=0)
                o16[0, c:c + cw, :] = acc.astype(BF16)


def _project(x, mod, row_of, w, wt, segs, tsegs, rope=None, rope_t=None, tm=512, cw=256, name="proj"):
    B, T, _ = x.shape
    tm = min(tm, T)
    cw = min([cw] + [s[1] for s in list(segs) + list(tsegs)])
    const = lambda b, i: (0, 0)
    in_specs = [pl.BlockSpec((1, tm, D_MODEL), lambda b, i: (b, i, 0)),
                pl.BlockSpec((1, 6, D_MODEL), lambda b, i: (row_of(b), 0, 0))]
    args = [x, mod]
    if segs:
        in_specs.append(pl.BlockSpec(w.shape, const))
        args.append(w)
    if tsegs:
        in_specs.append(pl.BlockSpec(wt.shape, const))
        args.append(wt)
    rope_n = 0
    if rope is not None:
        rope_n, tabs = rope
        in_specs += [pl.BlockSpec((tm, LANES), lambda b, i: (i, 0))] * 3
        args += list(tabs)
    if rope_t is not None:
        rope_n, tabs = rope_t
        in_specs += [pl.BlockSpec((LANES, tm), lambda b, i: (0, i))] * 3
        args += list(tabs)
    out_specs, out_shape = [], []
    for (_, width, _, want16, want32, _) in segs:
        for want, dt in ((want16, BF16), (want32, F32)):
            if want:
                out_specs.append(pl.BlockSpec((1, tm, width), lambda b, i: (b, i, 0)))
                out_shape.append(jax.ShapeDtypeStruct((B, T, width), dt))
    for (_, width, _, _, _, _) in tsegs:
        out_specs.append(pl.BlockSpec((1, width, tm), lambda b, i: (b, 0, i)))
        out_shape.append(jax.ShapeDtypeStruct((B, width, T), BF16))
    return pl.pallas_call(
        functools.partial(_proj_kernel, segs=tuple(segs), tsegs=tuple(tsegs), rope_n=rope_n, cw=cw),
        grid=(B, T // tm), in_specs=in_specs, out_specs=out_specs, out_shape=out_shape,
        compiler_params=_cparams(("parallel", "parallel")), name=name,
    )(*args)


def _mla_proj_kernel(*refs, lat, scale, cw):
    x_ref, mod_ref, wa_ref, qn_ref, kvn_ref, wuqt_ref, wk_ref, wvt_ref = refs[:8]
    pos = 8
    if lat:
        qcos, qsup, qsdn, kcos, ksup, ksdn = (r[...] for r in refs[8:14])
        pos = 14
    qt_out, k_out, vt_out = refs[pos:pos + 3]
    h = _modulated(x_ref, mod_ref).astype(BF16)
    a = _dot(h, wa_ref[...])
    cq = _rms_norm(a[:, :MLA_Q_RANK], qn_ref[...])
    ckv = _rms_norm(a[:, MLA_Q_RANK:MLA_Q_RANK + MLA_KV_RANK], kvn_ref[...])
    kpe = a[:, MLA_Q_RANK + MLA_KV_RANK:]
    if lat:
        kpe = _rope_slab(kpe, kcos, ksup, ksdn, MLA_ROPE // 4)
    else:
        ckv_state, kpe_state = refs[pos + 3:pos + 5]
        ckv_state[0] = ckv
        kpe_state[0] = kpe[:, :MLA_ROPE]
    cqt = cq.T.astype(BF16)
    n_q = wuqt_ref.shape[0]
    for c in range(0, n_q, cw):
        q = _dot(wuqt_ref[c:c + cw, :], cqt) * scale
        if lat:
            q = jnp.concatenate(
                [_rope_slab_t(q[s:s + LANES], qcos, qsup, qsdn, MLA_ROPE // 4) for s in range(0, cw, LANES)], axis=0)
        qt_out[0, c:c + cw, :] = q.astype(BF16)
    kin = jnp.concatenate([ckv.astype(BF16), kpe.astype(BF16)], axis=1)
    for c in range(0, wk_ref.shape[1], cw):
        k_out[0, :, c:c + cw] = _dot(kin, wk_ref[:, c:c + cw]).astype(BF16)
    ckvt = ckv.T.astype(BF16)
    for c in range(0, wvt_ref.shape[0], cw):
        vt_out[0, c:c + cw, :] = _dot(wvt_ref[c:c + cw, :], ckvt).astype(BF16)


def _mla_project(x, mod, row_of, wa, qn, kvn, wuqt, wk, wvt, scale, rope=None, tm=512, name="mla_proj"):
    B, T, _ = x.shape
    tm = min(tm, T)
    lat = rope is not None
    const = lambda b, i: (0, 0)
    in_specs = [pl.BlockSpec((1, tm, D_MODEL), lambda b, i: (b, i, 0)),
                pl.BlockSpec((1, 6, D_MODEL), lambda b, i: (row_of(b), 0, 0))]
    in_specs += [pl.BlockSpec(a.shape, const) for a in (wa, qn, kvn, wuqt, wk, wvt)]
    args = [x, mod, wa, qn, kvn, wuqt, wk, wvt]
    if lat:
        in_specs += [pl.BlockSpec((LANES, tm), lambda b, i: (0, i))] * 3
        in_specs += [pl.BlockSpec((tm, LANES), lambda b, i: (i, 0))] * 3
        args += list(rope)
    tok = lambda b, i: (b, i, 0)
    feat = lambda b, i: (b, 0, i)
    out_specs = [pl.BlockSpec((1, wuqt.shape[0], tm), feat), pl.BlockSpec((1, tm, wk.shape[1]), tok),
                 pl.BlockSpec((1, wvt.shape[0], tm), feat)]
    out_shape = [jax.ShapeDtypeStruct((B, wuqt.shape[0], T), BF16), jax.ShapeDtypeStruct((B, T, wk.shape[1]), BF16),
                 jax.ShapeDtypeStruct((B, wvt.shape[0], T), BF16)]
    if not lat:
        out_specs += [pl.BlockSpec((1, tm, MLA_KV_RANK), tok), pl.BlockSpec((1, tm, MLA_ROPE), tok)]
        out_shape += [jax.ShapeDtypeStruct((B, T, MLA_KV_RANK), F32), jax.ShapeDtypeStruct((B, T, MLA_ROPE), F32)]
    return pl.pallas_call(
        functools.partial(_mla_proj_kernel, lat=lat, scale=scale, cw=512),
        grid=(B, T // tm), in_specs=in_specs, out_specs=out_specs, out_shape=out_shape,
        compiler_params=_cparams(("parallel", "parallel")), name=name,
    )(*args)


def _mla_cache_kernel(ckv_ref, kpe_ref, wk_ref, wvt_ref, k_out, vt_out):
    ckv = ckv_ref[0]
    kin = jnp.concatenate([ckv.astype(BF16), kpe_ref[0].astype(BF16)], axis=1)
    k_out[0] = _dot(kin, wk_ref[...]).astype(BF16)
    vt_out[0] = _dot(wvt_ref[...], ckv.T.astype(BF16)).astype(BF16)


def _mla_cache_expand(c_ckv, c_kpe_slab, wk, wvt):
    B, P, _ = c_ckv.shape
    const = lambda b: (0, 0)
    return pl.pallas_call(
        _mla_cache_kernel, grid=(B,),
        in_specs=[pl.BlockSpec((1, P, MLA_KV_RANK), lambda b: (b, 0, 0)),
                  pl.BlockSpec((1, P, LANES), lambda b: (b, 0, 0)),
                  pl.BlockSpec(wk.shape, const), pl.BlockSpec(wvt.shape, const)],
        out_specs=[pl.BlockSpec((1, P, wk.shape[1]), lambda b: (b, 0, 0)),
                   pl.BlockSpec((1, wvt.shape[0], P), lambda b: (b, 0, 0))],
        out_shape=[jax.ShapeDtypeStruct((B, P, wk.shape[1]), BF16),
                   jax.ShapeDtypeStruct((B, wvt.shape[0], P), BF16)],
        compiler_params=_cparams(("parallel",)), name="mla_cache_expand",
    )(c_ckv, c_kpe_slab, wk, wvt)


def _flash_kernel(*refs, mode, has_cache, gpb, tq, tk):
    qt_ref = refs[0]
    pos = 1
    if has_cache:
        kc_ref, vct_ref = refs[1:3]
        pos = 3
    k_ref, vt_ref = refs[pos:pos + 2]
    pos += 2
    if mode == "diff":
        lam_ref, subln_ref = refs[pos:pos + 2]
        pos += 2
    o_ref, s_sc, acc_sc = refs[pos:pos + 3]
    dv = LANES if mode == "diff" else HALF
    wq = LANES if mode == "diff" else 2 * LANES

    blocks = []
    if has_cache:
        pc = kc_ref.shape[1]
        blocks += [(kc_ref, vct_ref, j, min(tk, pc - j)) for j in range(0, pc, tk)]
    blocks += [(k_ref, vt_ref, j, tk) for j in range(0, k_ref.shape[1], tk)]

    qts = [qt_ref[0, g * wq:(g + 1) * wq, :] for g in range(gpb)]
    if mode == "diff":
        qts = [_split_heads_t(qt) for qt in qts]

    def logits(n, g):
        kr, _, j, nk = blocks[n]
        k = kr[0, j:j + nk, g * wq:(g + 1) * wq]
        buf = 2 * g + n % 2
        if mode == "diff":
            s_sc[buf, :nk, :] = _dot(k, qts[g])
        else:
            s_sc[buf, :nk, :tq] = _dot(k[:, :LANES], qts[g][:LANES])
            s_sc[buf, :nk, tq:] = _dot(k[:, LANES:], qts[g][LANES:])

    for g in range(gpb):
        logits(0, g)
    m = [None] * gpb
    for n in range(len(blocks)):
        for g in range(gpb):
            if n + 1 < len(blocks):
                logits(n + 1, g)
            _, vr, j, nk = blocks[n]
            vt = vr[0, g * LANES:(g + 1) * LANES, j:j + nk]
            s = s_sc[2 * g + n % 2, :nk, :]
            m_cur = jnp.max(s, 0, keepdims=True)
            m_new = m_cur if m[g] is None else jnp.maximum(m[g], m_cur)
            pb = jnp.exp2(s - m_new).astype(BF16)
            if mode == "diff":
                pv = _dot(_with_ones_row(vt), pb)
            else:
                pv = jnp.concatenate([_dot(_with_ones_row(vt[:HALF]), pb[:, :tq]),
                                      _dot(_with_ones_row(vt[HALF:]), pb[:, tq:])], axis=1)
            if m[g] is None:
                acc_sc[g] = pv
            else:
                acc_sc[g] = jnp.exp2(m[g] - m_new) * acc_sc[g] + pv
            m[g] = m_new

    if mode == "diff":
        lam = lam_ref[...]
        lam_full = (jnp.exp(jnp.sum(lam[0:1] * lam[1:2], keepdims=True))
                    - jnp.exp(jnp.sum(lam[2:3] * lam[3:4], keepdims=True)) + DIFF_LAMBDA_INIT)
    for g in range(gpb):
        acc = acc_sc[g]
        o = acc[:dv] / acc[dv:dv + 1]
        o0, o1 = o[:, :tq], o[:, tq:]
        if mode == "diff":
            out_t = _rms_norm(o0 - lam_full * o1, subln_ref[...], axis=0) * (1.0 - DIFF_LAMBDA_INIT)
        else:
            out_t = jnp.concatenate([o0, o1], axis=0)
        o_ref[0, :, g * LANES:(g + 1) * LANES] = out_t.T.astype(o_ref.dtype)


def _flash(qt, k, vt, cache=None, *, mode, extra=(), gpb=1, tq=1024, tk=512, name="flash"):
    B, _, T = qt.shape
    S = k.shape[1]
    wq = LANES if mode == "diff" else 2 * LANES
    G = vt.shape[1] // LANES
    tq, tk = min(tq, T), min(tk, S)
    in_specs = [pl.BlockSpec((1, gpb * wq, tq), lambda b, g, i: (b, g, i))]
    args = [qt]
    if cache is not None:
        kc, vct = cache
        P = kc.shape[1]
        in_specs += [pl.BlockSpec((1, P, gpb * wq), lambda b, g, i: (b, 0, g)),
                     pl.BlockSpec((1, gpb * LANES, P), lambda b, g, i: (b, g, 0))]
        args += [kc, vct]
    in_specs += [pl.BlockSpec((1, S, gpb * wq), lambda b, g, i: (b, 0, g)),
                 pl.BlockSpec((1, gpb * LANES, S), lambda b, g, i: (b, g, 0))]
    args += [k, vt]
    for e in extra:
        in_specs.append(pl.BlockSpec(e.shape, lambda b, g, i: (0, 0)))
        args.append(e)
    dv = LANES if mode == "diff" else HALF
    return pl.pallas_call(
        functools.partial(_flash_kernel, mode=mode, has_cache=cache is not None, gpb=gpb, tq=tq, tk=tk),
        grid=(B, G // gpb, T // tq), in_specs=in_specs,
        out_specs=pl.BlockSpec((1, tq, gpb * LANES), lambda b, g, i: (b, i, g)),
        out_shape=jax.ShapeDtypeStruct((B, T, G * LANES), BF16),
        scratch_shapes=[pltpu.VMEM((2 * gpb, tk, 2 * tq), F32), pltpu.VMEM((gpb, dv + BF16_ROWS, 2 * tq), F32)],
        compiler_params=_cparams(("parallel", "parallel", "parallel")), name=name,
    )(*args)


def _band_kernel(*refs, n_seg, seg_bias, n_slab, gpb, has_bias, bias_groups, has_sink, tq):
    qt_ref = refs[0]
    kv = refs[1:1 + 2 * n_seg]
    pos = 1 + 2 * n_seg
    bias_ref = sink_ref = None
    if has_bias:
        bias_ref = refs[pos]
        pos += 1
    if has_sink:
        sink_ref = refs[pos]
        pos += 1
    o_ref, s_sc = refs[pos:pos + 2]
    rows = [kv[2 * i].shape[1] for i in range(n_seg)]
    offs = [sum(rows[:i]) for i in range(n_seg)]

    def logits(g):
        q0 = g * n_slab * LANES
        qst = jnp.concatenate(
            [_split_heads_t(qt_ref[0, q0 + s * LANES:q0 + (s + 1) * LANES, :]) for s in range(n_slab)], axis=1)
        for i in range(n_seg):
            lg = _dot(kv[2 * i][0, :, g * LANES:(g + 1) * LANES], qst)
            if seg_bias[i] is not None:
                lg = lg + bias_ref[0, g if bias_groups > 1 else 0, seg_bias[i]:seg_bias[i] + rows[i], :]
            s_sc[g % 2, offs[i]:offs[i] + rows[i], :] = lg

    logits(0)
    for g in range(gpb):
        if g + 1 < gpb:
            logits(g + 1)
        q0 = g * n_slab * LANES
        s = s_sc[g % 2]
        m = jnp.max(s, 0, keepdims=True)
        if has_sink:
            sink = sink_ref[g]
            m = jnp.maximum(m, sink)
        vt = jnp.concatenate([kv[2 * i + 1][0, g * LANES:(g + 1) * LANES, :] for i in range(n_seg)], axis=1)
        acc = _dot(_with_ones_row(vt), jnp.exp2(s - m).astype(BF16))
        l = acc[LANES:LANES + 1]
        if has_sink:
            l = l + jnp.exp2(sink - m)
        o = acc[:LANES] / l
        for s in range(n_slab):
            c = 2 * s * tq
            out_t = jnp.concatenate([o[:HALF, c:c + tq], o[HALF:, c + tq:c + 2 * tq]], axis=0)
            o_ref[0, :, q0 + s * LANES:q0 + (s + 1) * LANES] = out_t.T.astype(o_ref.dtype)


def _band(qt, segs, *, n_slab, tq, gpb=1, bias=None, sink=None, name="band"):
    B, _, T = qt.shape
    G = qt.shape[1] // (n_slab * LANES)
    nq = T // tq
    R = n_slab * 2 * tq
    in_specs = [pl.BlockSpec((1, gpb * n_slab * LANES, tq), lambda g, b, i: (b, g, i))]
    args = [qt]
    seg_bias = []
    for (k, vt, ts, blk_of, boff) in segs:
        in_specs += [pl.BlockSpec((1, ts, gpb * LANES), (lambda f: lambda g, b, i: (b, f(i, nq), g))(blk_of)),
                     pl.BlockSpec((1, gpb * LANES, ts), (lambda f: lambda g, b, i: (b, g, f(i, nq)))(blk_of))]
        args += [k, vt]
        seg_bias.append(boff)
    bias_groups = 1
    if bias is not None:
        n_var, bias_groups, sb, _ = bias.shape
        var_of = lambda i: jnp.where(i == 0, 0, jnp.where(i == nq - 1, n_var - 1, 1))
        nb = gpb if bias_groups > 1 else 1
        in_specs.append(pl.BlockSpec((1, nb, sb, R), lambda g, b, i: (var_of(i), g if bias_groups > 1 else 0, 0, 0)))
        args.append(bias)
    if sink is not None:
        in_specs.append(pl.BlockSpec((gpb, 1, R), lambda g, b, i: (g, 0, 0)))
        args.append(sink)
    return pl.pallas_call(
        functools.partial(_band_kernel, n_seg=len(segs), seg_bias=tuple(seg_bias), n_slab=n_slab, gpb=gpb,
                          has_bias=bias is not None, bias_groups=bias_groups, has_sink=sink is not None, tq=tq),
        grid=(G // gpb, B, nq), in_specs=in_specs,
        out_specs=pl.BlockSpec((1, tq, gpb * n_slab * LANES), lambda g, b, i: (b, i, g)),
        out_shape=jax.ShapeDtypeStruct((B, T, G * n_slab * LANES), BF16),
        scratch_shapes=[pltpu.VMEM((2, sum(s[2] for s in segs), R), F32)],
        compiler_params=_cparams(("parallel", "parallel", "parallel")), name=name,
    )(*args)


_OWN = lambda i, nq: i
_PREV = lambda i, nq: jnp.maximum(i - 1, 0)
_NEXT = lambda i, nq: jnp.minimum(i + 1, nq - 1)
_FIRST = lambda i, nq: 0


def _post_kernel(x_ref, o_ref, mod_ref, g_ref, b_ref, wo_ref, w1_ref, w2_ref, out_ref, *, cf, rows):
    subs = [slice(r, r + rows) for r in range(0, x_ref.shape[1], rows)]
    ys = [_dot(o_ref[0, s, :], wo_ref[...]) for s in subs]
    x1s = [_layer_norm(ALPHA * x_ref[0, s, :] + mod_ref[0, 2:3, :] * y, g_ref[0:1, :], b_ref[0:1, :])
           for s, y in zip(subs, ys)]
    hs = [(x1 * (1.0 + mod_ref[0, 4:5, :]) + mod_ref[0, 3:4, :]).astype(BF16) for x1 in x1s]
    accs = []
    for h in hs:
        acc = jnp.zeros(h.shape, F32)
        for c in range(0, D_FF, cf):
            u = jnp.maximum(_dot(h, w1_ref[:, c:c + cf]), 0.0)
            acc = acc + _dot((u * u).astype(BF16), w2_ref[c:c + cf, :])
        accs.append(acc)
    for s, x1, acc in zip(subs, x1s, accs):
        out_ref[0, s, :] = _layer_norm(ALPHA * x1 + mod_ref[0, 5:6, :] * acc, g_ref[1:2, :], b_ref[1:2, :])


def _post(x, o, mod, row_of, ln_g, ln_b, wo, w1, w2, tm=512, shared_mod=False, name="post"):
    shape = x.shape
    if shared_mod and shape[1] < tm and (shape[0] * shape[1]) % tm == 0:
        x, o = x.reshape(-1, tm, D_MODEL), o.reshape(-1, tm, D_MODEL)
    B, T, _ = x.shape
    tm = min(tm, T)
    const = lambda b, i: (0, 0)
    once = pl.Buffered(1)
    out = pl.pallas_call(
        functools.partial(_post_kernel, cf=1024, rows=min(tm, 256)),
        grid=(B, T // tm),
        in_specs=[pl.BlockSpec((1, tm, D_MODEL), lambda b, i: (b, i, 0)),
                  pl.BlockSpec((1, tm, D_MODEL), lambda b, i: (b, i, 0)),
                  pl.BlockSpec((1, 6, D_MODEL), lambda b, i: (row_of(b), 0, 0)),
                  pl.BlockSpec((2, D_MODEL), const), pl.BlockSpec((2, D_MODEL), const),
                  pl.BlockSpec(wo.shape, const, pipeline_mode=once),
                  pl.BlockSpec(w1.shape, const, pipeline_mode=once),
                  pl.BlockSpec(w2.shape, const, pipeline_mode=once)],
        out_specs=pl.BlockSpec((1, tm, D_MODEL), lambda b, i: (b, i, 0)),
        out_shape=jax.ShapeDtypeStruct(x.shape, F32),
        compiler_params=_cparams(("parallel", "parallel")), name=name,
    )(x, o, mod, ln_g, ln_b, wo, w1, w2)
    return out.reshape(shape)


def _rope_tables(T, R, lead, reps):
    n = R // 4
    t = jnp.arange(T)
    inv = ROPE_BASE ** (-jnp.arange(n, dtype=jnp.float32) / n)

    def cs(p):
        ang = p.astype(jnp.float32)[:, None] * inv
        return jnp.cos(ang), jnp.sin(ang)

    cr, sr = cs(t // GRID_W)
    cc, sc = cs(t % GRID_W)
    z = jnp.zeros_like(sr)
    cos = jnp.concatenate([cr, cr, cc, cc], -1)
    sup = jnp.concatenate([-sr, z, -sc, z], -1)
    sdn = jnp.concatenate([z, sr, z, sc], -1)
    tail = LANES - lead - reps * R

    def slab(a, fill):
        parts = [jnp.full((T, lead), fill, F32)] if lead else []
        parts += [a] * reps
        if tail:
            parts.append(jnp.full((T, tail), fill, F32))
        return jnp.concatenate(parts, -1)

    return slab(cos, 1.0), slab(sup, 0.0), slab(sdn, 0.0)


def _na_bias_tables(rpb, rows):
    qr, kb_n = NA_QROWS, 3 * NA_QROWS
    n_dr, n_dc = 2 * NA_KH - 1, 2 * NA_KW - 1
    c = np.arange(GRID_W)
    c0 = np.clip(c - NA_KW // 2, 0, GRID_W - NA_KW)
    kc = np.arange(GRID_W)[:, None]
    col_ok = (kc >= c0[None]) & (kc < c0[None] + NA_KW)
    sel_c = ((np.arange(n_dc)[:, None, None] == (kc - c[None] + NA_KW - 1)[None]) & col_ok[None])
    sel_r = np.zeros((3, n_dr, kb_n, qr), bool)
    for v, r_base in enumerate((0, qr, rows - qr)):
        r = r_base + np.arange(qr)[None]
        kr = r_base - qr + np.arange(kb_n)[:, None]
        r0 = np.clip(r - NA_KH // 2, 0, rows - NA_KH)
        row_ok = (kr >= r0) & (kr < r0 + NA_KH)
        sel_r[v] = (np.arange(n_dr)[:, None, None] == (kr - r + NA_KH - 1)[None]) & row_ok[None]
    hp = lax.Precision.HIGHEST
    H = rpb.shape[0]
    rpb2 = (rpb.astype(F32) * LOG2E).reshape(H // 2, 2, n_dr, n_dc)
    t1 = jnp.einsum("gxde,ejc->gxdjc", rpb2, jnp.asarray(sel_c, F32), precision=hp)
    tab = jnp.einsum("vdkr,gxdjc->vgkjxrc", jnp.asarray(sel_r, F32), t1, precision=hp)
    ok = np.einsum("vdkr,ejc->vkjrc", sel_r.astype(np.int32), sel_c.astype(np.int32)) > 0
    off = np.where(ok, 0.0, NEG_INF).astype(np.float32)[:, None, :, :, None]
    return (tab + jnp.asarray(off)).reshape(3, H // 2, kb_n * GRID_W, 2 * qr * GRID_W)


def _swa_mask_tables(nb, reps):
    qi = np.arange(SWA_QB)[None, :]
    kj = np.arange(3 * SWA_QB)[:, None]
    tabs = []
    for blk in (0, 1, nb - 1):
        kpos = (blk - 1) * SWA_QB + kj
        ok = (np.abs(kj - SWA_QB - qi) <= SWA_WINDOW) & (kpos >= 0) & (kpos < nb * SWA_QB)
        tabs.append(np.tile(np.where(ok, 0.0, NEG_INF).astype(np.float32), (1, reps))[None])
    return jnp.asarray(np.stack(tabs, 0))


_SWA_PERM = np.array([8 * p + 4 * half + s for p in range(2) for s in range(4) for half in range(2)])


def _perm_heads(w, axis, dh=HALF):
    shape = w.shape
    w = w.reshape(shape[:axis] + (SWA_HEADS, dh) + shape[axis + 1:])
    w = jnp.take(w, jnp.asarray(_SWA_PERM), axis=axis)
    return w.reshape(shape)


def kernel(x_prompt, x_sample, cache_l0_k, cache_l0_v, cache_l1_k, cache_l1_v, cache_l2_ckv, cache_l2_kpe, cache_l3_k, cache_l3_v, c, c_ctx, w_mod, b_mod, ln_g, ln_b, w_mlp1, w_mlp2, l0_w_qkv, l0_lam, l0_subln, l0_w_o, l1_w_qkv, l1_rpb, l1_w_o, l2_w_a, l2_q_norm, l2_kv_norm, l2_w_uq, l2_w_ukv, l2_w_o, l3_w_qkv, l3_sink, l3_w_o):
    D = D_MODEL
    Bc, S, _ = x_prompt.shape
    Bl, T, _ = x_sample.shape
    P = cache_l0_k.shape[1]
    assert 1 + Bl <= MOD_ROWS and T % GRID_W == 0
    rows = T // GRID_W

    cond = jnp.concatenate([c_ctx[None], c, jnp.zeros((MOD_ROWS - 1 - Bl, D), F32)], 0)
    mod = _modulation(cond, w_mod, b_mod)
    ctx_row = lambda b: 0
    lat_row = lambda b: b + 1

    bf = lambda a: a.astype(BF16)
    tr = lambda a: jnp.swapaxes(a, -1, -2)
    w1 = bf(w_mlp1)
    w2 = bf(w_mlp2)
    qmul = HALF ** -0.5 * LOG2E

    tabs64 = _rope_tables(T, DIFF_DH, 0, 2)
    rope64 = (DIFF_DH // 4, tabs64)
    rope64_t = (DIFF_DH // 4, tuple(tr(t) for t in tabs64))

    def qkv_t(x, mod_l, row_of, w_k, w_t, nq, nk, lat, rope, name):
        if lat:
            return _project(x, mod_l, row_of, w_k, w_t, [(0, nk, rope, True, False, None)],
                            [(0, nq, rope, True, False, qmul), (nq, nk, False, True, False, None)],
                            rope=rope64 if rope else None, rope_t=rope64_t if rope else None, name=name)
        return _project(x, mod_l, row_of, w_k, w_t,
                        [(0, nk, False, True, True, None), (nk, nk, False, False, True, None)],
                        [(0, nq, False, True, False, qmul), (nq, nk, False, True, False, None)], name=name)

    def split_w(w, nq, nk):
        return bf(w[:, nq:]), bf(tr(jnp.concatenate([w[:, :nq], w[:, nq + nk:]], 1)))

    w_k, w_t = split_w(l0_w_qkv, D, D)
    wo = bf(l0_w_o)
    lam = l0_lam.astype(F32)
    subln = l0_subln.reshape(2 * DIFF_DH, 1)
    k, k0, v0, qt, vt = qkv_t(x_prompt, mod[0], ctx_row, w_k, w_t, D, D, False, False, "l0_proj_ctx")
    o = _flash(qt, k, vt, mode="diff", extra=(lam, subln), gpb=DIFF_HEADS, name="l0_attn_ctx")
    xp = _post(x_prompt, o, mod[0], ctx_row, ln_g[0], ln_b[0], wo, w1[0], w2[0], shared_mod=True, name="l0_post_ctx")
    k, qt, vt = qkv_t(x_sample, mod[0], lat_row, w_k, w_t, D, D, True, True, "l0_proj_lat")
    cache = (bf(cache_l0_k.reshape(Bl, P, D)), bf(tr(cache_l0_v.reshape(Bl, P, D))))
    o = _flash(qt, k, vt, cache, mode="diff", extra=(lam, subln), gpb=2, tq=512, name="l0_attn_lat")
    xs = _post(x_sample, o, mod[0], lat_row, ln_g[0], ln_b[0], wo, w1[0], w2[0], name="l0_post_lat")

    w_k, w_t = split_w(l1_w_qkv, D, D)
    wo = bf(l1_w_o)
    k, k1, v1, qt, vt = qkv_t(xp, mod[1], ctx_row, w_k, w_t, D, D, False, False, "l1_proj_ctx")
    o = _band(qt, [(k, vt, S, _OWN, None)], n_slab=1, tq=S, gpb=NA_HEADS // 2, name="l1_attn_ctx")
    xp = _post(xp, o, mod[1], ctx_row, ln_g[1], ln_b[1], wo, w1[1], w2[1], shared_mod=True, name="l1_post_ctx")
    k, qt, vt = qkv_t(xs, mod[1], lat_row, w_k, w_t, D, D, True, False, "l1_proj_lat")
    kc, vct = bf(cache_l1_k.reshape(Bl, P, D)), bf(tr(cache_l1_v.reshape(Bl, P, D)))
    tqn = NA_QROWS * GRID_W
    bias = _na_bias_tables(l1_rpb, rows)
    o = _band(qt, [(k, vt, tqn, _PREV, 0), (k, vt, tqn, _OWN, tqn), (k, vt, tqn, _NEXT, 2 * tqn),
                   (kc, vct, P, _FIRST, None)], n_slab=1, tq=tqn, gpb=NA_HEADS // 2, bias=bias, name="l1_attn_lat")
    xs = _post(xs, o, mod[1], lat_row, ln_g[1], ln_b[1], wo, w1[1], w2[1], name="l1_post_lat")

    H, dq = MLA_HEADS, MLA_NOPE + MLA_ROPE
    wa = bf(jnp.pad(l2_w_a, ((0, 0), (0, 7 * LANES - l2_w_a.shape[1]))))
    wuq = jnp.pad(l2_w_uq.reshape(MLA_Q_RANK, H, dq), ((0, 0), (0, 0), (0, LANES - dq))).reshape(MLA_Q_RANK, H * LANES)
    wuqt = bf(tr(wuq))
    wukv = l2_w_ukv.reshape(MLA_KV_RANK, H, MLA_NOPE + MLA_VDIM)
    wk_n = jnp.pad(wukv[:, :, :MLA_NOPE], ((0, 0), (0, 0), (0, LANES - MLA_NOPE))).reshape(MLA_KV_RANK, H * LANES)
    place = np.zeros((LANES, H, LANES), np.float32)
    place[np.arange(MLA_ROPE), :, MLA_NOPE + np.arange(MLA_ROPE)] = 1.0
    wk = bf(jnp.concatenate([wk_n, jnp.asarray(place.reshape(LANES, H * LANES))], 0))
    wvt = bf(tr(wukv[:, :, MLA_NOPE:].reshape(MLA_KV_RANK, H * MLA_VDIM)))
    wo = bf(l2_w_o)
    qn, kvn = l2_q_norm.reshape(1, -1), l2_kv_norm.reshape(1, -1)
    mscale = dq ** -0.5 * LOG2E
    qt, k, vt, ckv2, kpe2 = _mla_project(xp, mod[2], ctx_row, wa, qn, kvn, wuqt, wk, wvt, mscale, name="l2_proj_ctx")
    o = _flash(qt, k, vt, mode="mla", gpb=MLA_HEADS // 2, name="l2_attn_ctx")
    xp = _post(xp, o, mod[2], ctx_row, ln_g[2], ln_b[2], wo, w1[2], w2[2], shared_mod=True, name="l2_post_ctx")
    tabs = tuple(tr(t) for t in _rope_tables(T, MLA_ROPE, MLA_NOPE, 1)) + _rope_tables(T, MLA_ROPE, 0, 1)
    qt, k, vt = _mla_project(xs, mod[2], lat_row, wa, qn, kvn, wuqt, wk, wvt, mscale, rope=tabs, name="l2_proj_lat")
    kpe_slab = jnp.pad(cache_l2_kpe, ((0, 0), (0, 0), (0, LANES - MLA_ROPE)))
    cache = _mla_cache_expand(cache_l2_ckv, kpe_slab, wk, wvt)
    o = _flash(qt, k, vt, cache, mode="mla", tk=256, name="l2_attn_lat")
    xs = _post(xs, o, mod[2], lat_row, ln_g[2], ln_b[2], wo, w1[2], w2[2], name="l2_post_lat")

    nq3, nk3 = SWA_HEADS * HALF, SWA_KV_HEADS * HALF
    w = jnp.concatenate([_perm_heads(l3_w_qkv[:, :nq3], 1), l3_w_qkv[:, nq3:]], 1)
    w_k, w_t = split_w(w, nq3, nk3)
    wo = bf(_perm_heads(l3_w_o, 0))
    sink = jnp.take(l3_sink.astype(F32), jnp.asarray(_SWA_PERM)) * LOG2E

    def sink_rows(tq):
        return jnp.repeat(sink.reshape(2, 1, 8), tq, axis=-1)

    k, k3, v3, qt, vt = qkv_t(xp, mod[3], ctx_row, w_k, w_t, nq3, nk3, False, False, "l3_proj_ctx")
    o = _band(qt, [(k, vt, S, _OWN, None)], n_slab=4, tq=S, gpb=2, sink=sink_rows(S), name="l3_attn_ctx")
    xp = _post(xp, o, mod[3], ctx_row, ln_g[3], ln_b[3], wo, w1[3], w2[3], shared_mod=True, name="l3_post_ctx")
    k, qt, vt = qkv_t(xs, mod[3], lat_row, w_k, w_t, nq3, nk3, True, True, "l3_proj_lat")
    kc, vct = bf(cache_l3_k.reshape(Bl, P, nk3)), bf(tr(cache_l3_v.reshape(Bl, P, nk3)))
    mask = _swa_mask_tables(T // SWA_QB, 8)
    o = _band(qt, [(k, vt, SWA_QB, _PREV, 0), (k, vt, SWA_QB, _OWN, SWA_QB), (k, vt, SWA_QB, _NEXT, 2 * SWA_QB),
                   (kc, vct, P, _FIRST, None)], n_slab=4, tq=SWA_QB, gpb=2, bias=mask, sink=sink_rows(SWA_QB),
              name="l3_attn_lat")
    xs = _post(xs, o, mod[3], lat_row, ln_g[3], ln_b[3], wo, w1[3], w2[3], name="l3_post_lat")

    return (xp, xs,
            k0.reshape(Bc, S, DIFF_HEADS, 2 * DIFF_DH), v0.reshape(Bc, S, DIFF_HEADS, 2 * DIFF_DH),
            k1.reshape(Bc, S, NA_HEADS, HALF), v1.reshape(Bc, S, NA_HEADS, HALF),
            ckv2, kpe2,
            k3.reshape(Bc, S, SWA_KV_HEADS, HALF), v3.reshape(Bc, S, SWA_KV_HEADS, HALF))
```

```python
import functools
import math

import numpy as np
import jax
import jax.numpy as jnp
from jax import lax
from jax.experimental import pallas as pl
from jax.experimental.pallas import tpu as pltpu

D_MODEL = 1024
DEPTH = 4
GRID_W = 64
ROPE_BASE = 10000.0
LN_EPS = 1e-5
NEG_INF = -1e30
ALPHA = (2 * DEPTH) ** 0.25
D_FF = 4 * D_MODEL
DIFF_DH = 64
DIFF_HEADS = 8
DIFF_LAMBDA_INIT = 0.8 - 0.6 * math.exp(-0.3 * 0)
NA_HEADS = 16
NA_KH = 8
NA_KW = 16
NA_QROWS = 4
MLA_HEADS = 16
MLA_NOPE = 64
MLA_ROPE = 32
MLA_VDIM = 64
MLA_Q_RANK = 512
MLA_KV_RANK = 256
SWA_HEADS = 16
SWA_KV_HEADS = 4
SWA_WINDOW = 128
SWA_QB = 128
LOG2E = math.log2(math.e)

LANES = 128
HALF = LANES // 2
BF16_ROWS = 16
MOD_ROWS = 16
VMEM_LIMIT = 56 * 1024 * 1024

F32 = jnp.float32
BF16 = jnp.bfloat16


def _cparams(sem):
    return pltpu.CompilerParams(dimension_semantics=sem, vmem_limit_bytes=VMEM_LIMIT)


def _dot(a, b):
    return jnp.dot(a, b, preferred_element_type=F32)


def _layer_norm(x, g, b):
    mu = jnp.mean(x, -1, keepdims=True)
    xc = x - mu
    var = jnp.mean(xc * xc, -1, keepdims=True)
    return xc * lax.rsqrt(var + LN_EPS) * g + b


def _rms_norm(x, g, axis=-1):
    return x * lax.rsqrt(jnp.mean(x * x, axis, keepdims=True) + LN_EPS) * g


def _rope_slab(x, cos, sin_up, sin_dn, n):
    return x * cos + pltpu.roll(x, LANES - n, 1) * sin_up + pltpu.roll(x, n, 1) * sin_dn


def _rope_slab_t(x, cos, sin_up, sin_dn, n):
    up = jnp.concatenate([x[n:], x[:n]], axis=0)
    dn = jnp.concatenate([x[LANES - n:], x[:LANES - n]], axis=0)
    return x * cos + up * sin_up + dn * sin_dn


def _split_heads_t(qt):
    lo = lax.broadcasted_iota(jnp.int32, qt.shape, 0) < HALF
    zero = jnp.zeros_like(qt)
    return jnp.concatenate([jnp.where(lo, qt, zero), jnp.where(lo, zero, qt)], axis=1)


def _with_ones_row(vt):
    r = lax.broadcasted_iota(jnp.int32, (BF16_ROWS, vt.shape[1]), 0)
    return jnp.concatenate([vt, jnp.where(r == 0, 1.0, 0.0).astype(BF16)], axis=0)


def _modulated(x_ref, mod_ref):
    return x_ref[0] * (1.0 + mod_ref[0, 1:2, :]) + mod_ref[0, 0:1, :]


def _mod_kernel(c_ref, w_ref, b_ref, o_ref):
    c = c_ref[...]
    s = c / (1.0 + jnp.exp(-c))
    o_ref[0] = _dot(s.astype(BF16), w_ref[0].astype(BF16)) + b_ref[0]


def _modulation(cond, w_mod, b_mod):
    tn = 2048
    n = 6 * D_MODEL
    out = pl.pallas_call(
        _mod_kernel,
        grid=(DEPTH, n // tn),
        in_specs=[pl.BlockSpec((MOD_ROWS, D_MODEL), lambda l, j: (0, 0)),
                  pl.BlockSpec((1, D_MODEL, tn), lambda l, j: (l, 0, j)),
                  pl.BlockSpec((1, 1, tn), lambda l, j: (l, 0, j))],
        out_specs=pl.BlockSpec((1, MOD_ROWS, tn), lambda l, j: (l, 0, j)),
        out_shape=jax.ShapeDtypeStruct((DEPTH, MOD_ROWS, n), F32),
        compiler_params=_cparams(("parallel", "parallel")),
        name="modulation",
    )(cond, w_mod, b_mod.reshape(DEPTH, 1, n))
    return out.reshape(DEPTH, MOD_ROWS, 6, D_MODEL)


def _proj_kernel(*refs, segs, tsegs, rope_n, cw):
    x_ref, mod_ref = refs[:2]
    pos = 2
    w_ref = wt_ref = None
    if segs:
        w_ref = refs[pos]
        pos += 1
    if tsegs:
        wt_ref = refs[pos]
        pos += 1
    if rope_n and any(s[2] for s in segs):
        cos, sup, sdn = (r[...] for r in refs[pos:pos + 3])
        pos += 3
    if rope_n and any(s[2] for s in tsegs):
        cos_t, sup_t, sdn_t = (r[...] for r in refs[pos:pos + 3])
        pos += 3
    outs = list(refs[pos:])
    h32 = _modulated(x_ref, mod_ref)
    h = h32.astype(BF16)
    for (start, width, rope, want16, want32, mul) in segs:
        o16 = outs.pop(0) if want16 else None
        o32 = outs.pop(0) if want32 else None
        for c in range(0, width, cw):
            acc = _dot(h, w_ref[:, start + c:start + c + cw])
            if want32:
                o32[0, :, c:c + cw] = acc
            if mul is not None:
                acc = acc * mul
            if rope:
                acc = jnp.concatenate(
                    [_rope_slab(acc[:, s:s + LANES], cos, sup, sdn, rope_n) for s in range(0, cw, LANES)], axis=1)
            if want16:
                o16[0, :, c:c + cw] = acc.astype(BF16)
    if tsegs:
        ht = h32.T.astype(BF16)
        for (start, width, rope, _, _, mul) in tsegs:
            o16 = outs.pop(0)
            for c in range(0, width, cw):
                acc = _dot(wt_ref[start + c:start + c + cw, :], ht)
                if mul is not None:
                    acc = acc * mul
                if rope:
                    acc = jnp.concatenate(
                        [_rope_slab_t(acc[s:s + LANES], cos_t, sup_t, sdn_t, rope_n) for s in range(0, cw, LANES)],
                        axis=0)
                o16[0, c:c + cw, :] = acc.astype(BF16)


def _project(x, mod, row_of, w, wt, segs, tsegs, rope=None, rope_t=None, tm=512, cw=256, name="proj"):
    B, T, _ = x.shape
    tm = min(tm, T)
    cw = min([cw] + [s[1] for s in list(segs) + list(tsegs)])
    const = lambda b, i: (0, 0)
    in_specs = [pl.BlockSpec((1, tm, D_MODEL), lambda b, i: (b, i, 0)),
                pl.BlockSpec((1, 6, D_MODEL), lambda b, i: (row_of(b), 0, 0))]
    args = [x, mod]
    if segs:
        in_specs.append(pl.BlockSpec(w.shape, const))
        args.append(w)
    if tsegs:
        in_specs.append(pl.BlockSpec(wt.shape, const))
        args.append(wt)
    rope_n = 0
    if rope is not None:
        rope_n, tabs = rope
        in_specs += [pl.BlockSpec((tm, LANES), lambda b, i: (i, 0))] * 3
        args += list(tabs)
    if rope_t is not None:
        rope_n, tabs = rope_t
        in_specs += [pl.BlockSpec((LANES, tm), lambda b, i: (0, i))] * 3
        args += list(tabs)
    out_specs, out_shape = [], []
    for (_, width, _, want16, want32, _) in segs:
        for want, dt in ((want16, BF16), (want32, F32)):
            if want:
                out_specs.append(pl.BlockSpec((1, tm, width), lambda b, i: (b, i, 0)))
                out_shape.append(jax.ShapeDtypeStruct((B, T, width), dt))
    for (_, width, _, _, _, _) in tsegs:
        out_specs.append(pl.BlockSpec((1, width, tm), lambda b, i: (b, 0, i)))
        out_shape.append(jax.ShapeDtypeStruct((B, width, T), BF16))
    return pl.pallas_call(
        functools.partial(_proj_kernel, segs=tuple(segs), tsegs=tuple(tsegs), rope_n=rope_n, cw=cw),
        grid=(B, T // tm), in_specs=in_specs, out_specs=out_specs, out_shape=out_shape,
        compiler_params=_cparams(("parallel", "parallel")), name=name,
    )(*args)


def _mla_proj_kernel(*refs, lat, scale, cw):
    x_ref, mod_ref, wa_ref, qn_ref, kvn_ref, wuqt_ref, wk_ref, wvt_ref = refs[:8]
    pos = 8
    if lat:
        qcos, qsup, qsdn, kcos, ksup, ksdn = (r[...] for r in refs[8:14])
        pos = 14
    qt_out, k_out, vt_out = refs[pos:pos + 3]
    h = _modulated(x_ref, mod_ref).astype(BF16)
    a = _dot(h, wa_ref[...])
    cq = _rms_norm(a[:, :MLA_Q_RANK], qn_ref[...])
    ckv = _rms_norm(a[:, MLA_Q_RANK:MLA_Q_RANK + MLA_KV_RANK], kvn_ref[...])
    kpe = a[:, MLA_Q_RANK + MLA_KV_RANK:]
    if lat:
        kpe = _rope_slab(kpe, kcos, ksup, ksdn, MLA_ROPE // 4)
    else:
        ckv_state, kpe_state = refs[pos + 3:pos + 5]
        ckv_state[0] = ckv
        kpe_state[0] = kpe[:, :MLA_ROPE]
    cqt = cq.T.astype(BF16)
    n_q = wuqt_ref.shape[0]
    for c in range(0, n_q, cw):
        q = _dot(wuqt_ref[c:c + cw, :], cqt) * scale
        if lat:
            q = jnp.concatenate(
                [_rope_slab_t(q[s:s + LANES], qcos, qsup, qsdn, MLA_ROPE // 4) for s in range(0, cw, LANES)], axis=0)
        qt_out[0, c:c + cw, :] = q.astype(BF16)
    kin = jnp.concatenate([ckv.astype(BF16), kpe.astype(BF16)], axis=1)
    for c in range(0, wk_ref.shape[1], cw):
        k_out[0, :, c:c + cw] = _dot(kin, wk_ref[:, c:c + cw]).astype(BF16)
    ckvt = ckv.T.astype(BF16)
    for c in range(0, wvt_ref.shape[0], cw):
        vt_out[0, c:c + cw, :] = _dot(wvt_ref[c:c + cw, :], ckvt).astype(BF16)


def _mla_project(x, mod, row_of, wa, qn, kvn, wuqt, wk, wvt, scale, rope=None, tm=512, name="mla_proj"):
    B, T, _ = x.shape
    tm = min(tm, T)
    lat = rope is not None
    const = lambda b, i: (0, 0)
    in_specs = [pl.BlockSpec((1, tm, D_MODEL), lambda b, i: (b, i, 0)),
                pl.BlockSpec((1, 6, D_MODEL), lambda b, i: (row_of(b), 0, 0))]
    in_specs += [pl.BlockSpec(a.shape, const) for a in (wa, qn, kvn, wuqt, wk, wvt)]
    args = [x, mod, wa, qn, kvn, wuqt, wk, wvt]
    if lat:
        in_specs += [pl.BlockSpec((LANES, tm), lambda b, i: (0, i))] * 3
        in_specs += [pl.BlockSpec((tm, LANES), lambda b, i: (i, 0))] * 3
        args += list(rope)
    tok = lambda b, i: (b, i, 0)
    feat = lambda b, i: (b, 0, i)
    out_specs = [pl.BlockSpec((1, wuqt.shape[0], tm), feat), pl.BlockSpec((1, tm, wk.shape[1]), tok),
                 pl.BlockSpec((1, wvt.shape[0], tm), feat)]
    out_shape = [jax.ShapeDtypeStruct((B, wuqt.shape[0], T), BF16), jax.ShapeDtypeStruct((B, T, wk.shape[1]), BF16),
                 jax.ShapeDtypeStruct((B, wvt.shape[0], T), BF16)]
    if not lat:
        out_specs += [pl.BlockSpec((1, tm, MLA_KV_RANK), tok), pl.BlockSpec((1, tm, MLA_ROPE), tok)]
        out_shape += [jax.ShapeDtypeStruct((B, T, MLA_KV_RANK), F32), jax.ShapeDtypeStruct((B, T, MLA_ROPE), F32)]
    return pl.pallas_call(
        functools.partial(_mla_proj_kernel, lat=lat, scale=scale, cw=512),
        grid=(B, T // tm), in_specs=in_specs, out_specs=out_specs, out_shape=out_shape,
        compiler_params=_cparams(("parallel", "parallel")), name=name,
    )(*args)


def _mla_cache_kernel(ckv_ref, kpe_ref, wk_ref, wvt_ref, k_out, vt_out):
    ckv = ckv_ref[0]
    kin = jnp.concatenate([ckv.astype(BF16), kpe_ref[0].astype(BF16)], axis=1)
    k_out[0] = _dot(kin, wk_ref[...]).astype(BF16)
    vt_out[0] = _dot(wvt_ref[...], ckv.T.astype(BF16)).astype(BF16)


def _mla_cache_expand(c_ckv, c_kpe_slab, wk, wvt):
    B, P, _ = c_ckv.shape
    const = lambda b: (0, 0)
    return pl.pallas_call(
        _mla_cache_kernel, grid=(B,),
        in_specs=[pl.BlockSpec((1, P, MLA_KV_RANK), lambda b: (b, 0, 0)),
                  pl.BlockSpec((1, P, LANES), lambda b: (b, 0, 0)),
                  pl.BlockSpec(wk.shape, const), pl.BlockSpec(wvt.shape, const)],
        out_specs=[pl.BlockSpec((1, P, wk.shape[1]), lambda b: (b, 0, 0)),
                   pl.BlockSpec((1, wvt.shape[0], P), lambda b: (b, 0, 0))],
        out_shape=[jax.ShapeDtypeStruct((B, P, wk.shape[1]), BF16),
                   jax.ShapeDtypeStruct((B, wvt.shape[0], P), BF16)],
        compiler_params=_cparams(("parallel",)), name="mla_cache_expand",
    )(c_ckv, c_kpe_slab, wk, wvt)


def _flash_kernel(*refs, mode, has_cache, gpb, tq, tk):
    qt_ref = refs[0]
    pos = 1
    if has_cache:
        kc_ref, vct_ref = refs[1:3]
        pos = 3
    k_ref, vt_ref = refs[pos:pos + 2]
    pos += 2
    if mode == "diff":
        lam_ref, subln_ref = refs[pos:pos + 2]
        pos += 2
    o_ref, s_sc, acc_sc = refs[pos:pos + 3]
    dv = LANES if mode == "diff" else HALF
    wq = LANES if mode == "diff" else 2 * LANES

    blocks = []
    if has_cache:
        pc = kc_ref.shape[1]
        blocks += [(kc_ref, vct_ref, j, min(tk, pc - j)) for j in range(0, pc, tk)]
    blocks += [(k_ref, vt_ref, j, tk) for j in range(0, k_ref.shape[1], tk)]

    qts = [qt_ref[0, g * wq:(g + 1) * wq, :] for g in range(gpb)]
    if mode == "diff":
        qts = [_split_heads_t(qt) for qt in qts]

    def logits(n, g):
        kr, _, j, nk = blocks[n]
        k = kr[0, j:j + nk, g * wq:(g + 1) * wq]
        buf = 2 * g + n % 2
        if mode == "diff":
            s_sc[buf, :nk, :] = _dot(k, qts[g])
        else:
            s_sc[buf, :nk, :tq] = _dot(k[:, :LANES], qts[g][:LANES])
            s_sc[buf, :nk, tq:] = _dot(k[:, LANES:], qts[g][LANES:])

    for g in range(gpb):
        logits(0, g)
    m = [None] * gpb
    for n in range(len(blocks)):
        for g in range(gpb):
            if n + 1 < len(blocks):
                logits(n + 1, g)
            _, vr, j, nk = blocks[n]
            vt = vr[0, g * LANES:(g + 1) * LANES, j:j + nk]
            s = s_sc[2 * g + n % 2, :nk, :]
            m_cur = jnp.max(s, 0, keepdims=True)
            m_new = m_cur if m[g] is None else jnp.maximum(m[g], m_cur)
            pb = jnp.exp2(s - m_new).astype(BF16)
            if mode == "diff":
                pv = _dot(_with_ones_row(vt), pb)
            else:
                pv = jnp.concatenate([_dot(_with_ones_row(vt[:HALF]), pb[:, :tq]),
                                      _dot(_with_ones_row(vt[HALF:]), pb[:, tq:])], axis=1)
            if m[g] is None:
                acc_sc[g] = pv
            else:
                acc_sc[g] = jnp.exp2(m[g] - m_new) * acc_sc[g] + pv
            m[g] = m_new

    if mode == "diff":
        lam = lam_ref[...]
        lam_full = (jnp.exp(jnp.sum(lam[0:1] * lam[1:2], keepdims=True))
                    - jnp.exp(jnp.sum(lam[2:3] * lam[3:4], keepdims=True)) + DIFF_LAMBDA_INIT)
    for g in range(gpb):
        acc = acc_sc[g]
        o = acc[:dv] / acc[dv:dv + 1]
        o0, o1 = o[:, :tq], o[:, tq:]
        if mode == "diff":
            out_t = _rms_norm(o0 - lam_full * o1, subln_ref[...], axis=0) * (1.0 - DIFF_LAMBDA_INIT)
        else:
            out_t = jnp.concatenate([o0, o1], axis=0)
        o_ref[0, :, g * LANES:(g + 1) * LANES] = out_t.T.astype(o_ref.dtype)


def _flash(qt, k, vt, cache=None, *, mode, extra=(), gpb=1, tq=1024, tk=512, name="flash"):
    B, _, T = qt.shape
    S = k.shape[1]
    wq = LANES if mode == "diff" else 2 * LANES
    G = vt.shape[1] // LANES
    tq, tk = min(tq, T), min(tk, S)
    in_specs = [pl.BlockSpec((1, gpb * wq, tq), lambda b, g, i: (b, g, i))]
    args = [qt]
    if cache is not None:
        kc, vct = cache
        P = kc.shape[1]
        in_specs += [pl.BlockSpec((1, P, gpb * wq), lambda b, g, i: (b, 0, g)),
                     pl.BlockSpec((1, gpb * LANES, P), lambda b, g, i: (b, g, 0))]
        args += [kc, vct]
    in_specs += [pl.BlockSpec((1, S, gpb * wq), lambda b, g, i: (b, 0, g)),
                 pl.BlockSpec((1, gpb * LANES, S), lambda b, g, i: (b, g, 0))]
    args += [k, vt]
    for e in extra:
        in_specs.append(pl.BlockSpec(e.shape, lambda b, g, i: (0, 0)))
        args.append(e)
    dv = LANES if mode == "diff" else HALF
    return pl.pallas_call(
        functools.partial(_flash_kernel, mode=mode, has_cache=cache is not None, gpb=gpb, tq=tq, tk=tk),
        grid=(B, G // gpb, T // tq), in_specs=in_specs,
        out_specs=pl.BlockSpec((1, tq, gpb * LANES), lambda b, g, i: (b, i, g)),
        out_shape=jax.ShapeDtypeStruct((B, T, G * LANES), BF16),
        scratch_shapes=[pltpu.VMEM((2 * gpb, tk, 2 * tq), F32), pltpu.VMEM((gpb, dv + BF16_ROWS, 2 * tq), F32)],
        compiler_params=_cparams(("parallel", "parallel", "parallel")), name=name,
    )(*args)


def _band_kernel(*refs, n_seg, seg_bias, n_slab, gpb, bias_kind, has_sink, tq):
    qt_ref = refs[0]
    kv = refs[1:1 + 2 * n_seg]
    pos = 1 + 2 * n_seg
    sink_ref = None
    if bias_kind == "table":
        bias_ref = refs[pos]
        pos += 1
    elif bias_kind == "na":
        blk_ref, rowok_ref = refs[pos:pos + 2]
        pos += 2
    if has_sink:
        sink_ref = refs[pos]
        pos += 1
    o_ref, s_sc = refs[pos:pos + 2]
    rows = [kv[2 * i].shape[1] for i in range(n_seg)]
    offs = [sum(rows[:i]) for i in range(n_seg)]

    def bias(g, off, n):
        if bias_kind == "table":
            return bias_ref[0, 0, off:off + n, :]
        tiles = []
        for kb in range(off // GRID_W, (off + n) // GRID_W):
            tiles.append(jnp.concatenate(
                [blk_ref[g, x, kb - 2 * p + 2] + rowok_ref[0, 2 * kb + p:2 * kb + p + 1, :]
                 for x in range(2) for p in range(NA_QROWS // 2)], axis=1))
        return jnp.concatenate(tiles, axis=0)

    def logits(g):
        q0 = g * n_slab * LANES
        qst = jnp.concatenate(
            [_split_heads_t(qt_ref[0, q0 + s * LANES:q0 + (s + 1) * LANES, :]) for s in range(n_slab)], axis=1)
        for i in range(n_seg):
            lg = _dot(kv[2 * i][0, :, g * LANES:(g + 1) * LANES], qst)
            if seg_bias[i] is not None:
                lg = lg + bias(g, seg_bias[i], rows[i])
            s_sc[g % 2, offs[i]:offs[i] + rows[i], :] = lg

    logits(0)
    for g in range(gpb):
        if g + 1 < gpb:
            logits(g + 1)
        q0 = g * n_slab * LANES
        s = s_sc[g % 2]
        m = jnp.max(s, 0, keepdims=True)
        if has_sink:
            sink = sink_ref[g]
            m = jnp.maximum(m, sink)
        vt = jnp.concatenate([kv[2 * i + 1][0, g * LANES:(g + 1) * LANES, :] for i in range(n_seg)], axis=1)
        acc = _dot(_with_ones_row(vt), jnp.exp2(s - m).astype(BF16))
        l = acc[LANES:LANES + 1]
        if has_sink:
            l = l + jnp.exp2(sink - m)
        o = acc[:LANES] / l
        for s in range(n_slab):
            c = 2 * s * tq
            out_t = jnp.concatenate([o[:HALF, c:c + tq], o[HALF:, c + tq:c + 2 * tq]], axis=0)
            o_ref[0, :, q0 + s * LANES:q0 + (s + 1) * LANES] = out_t.T.astype(o_ref.dtype)


def _band(qt, segs, *, n_slab, tq, gpb=1, bias=None, na_bias=None, sink=None, name="band"):
    B, _, T = qt.shape
    G = qt.shape[1] // (n_slab * LANES)
    nq = T // tq
    R = n_slab * 2 * tq
    in_specs = [pl.BlockSpec((1, gpb * n_slab * LANES, tq), lambda g, b, i: (b, g, i))]
    args = [qt]
    seg_bias = []
    for (k, vt, ts, blk_of, boff) in segs:
        in_specs += [pl.BlockSpec((1, ts, gpb * LANES), (lambda f: lambda g, b, i: (b, f(i, nq), g))(blk_of)),
                     pl.BlockSpec((1, gpb * LANES, ts), (lambda f: lambda g, b, i: (b, g, f(i, nq)))(blk_of))]
        args += [k, vt]
        seg_bias.append(boff)
    var_of = lambda i: jnp.where(i == 0, 0, jnp.where(i == nq - 1, 2, 1))
    bias_kind = None
    if bias is not None:
        bias_kind = "table"
        in_specs.append(pl.BlockSpec((1, 1, bias.shape[2], R), lambda g, b, i: (var_of(i), 0, 0, 0)))
        args.append(bias)
    elif na_bias is not None:
        bias_kind = "na"
        blk, rowok = na_bias
        in_specs += [pl.BlockSpec((gpb,) + blk.shape[1:], lambda g, b, i: (g, 0, 0, 0, 0)),
                     pl.BlockSpec((1,) + rowok.shape[1:], lambda g, b, i: (var_of(i), 0, 0))]
        args += [blk, rowok]
    if sink is not None:
        in_specs.append(pl.BlockSpec((gpb, 1, R), lambda g, b, i: (g, 0, 0)))
        args.append(sink)
    return pl.pallas_call(
        functools.partial(_band_kernel, n_seg=len(segs), seg_bias=tuple(seg_bias), n_slab=n_slab, gpb=gpb,
                          bias_kind=bias_kind, has_sink=sink is not None, tq=tq),
        grid=(G // gpb, B, nq), in_specs=in_specs,
        out_specs=pl.BlockSpec((1, tq, gpb * n_slab * LANES), lambda g, b, i: (b, i, g)),
        out_shape=jax.ShapeDtypeStruct((B, T, G * n_slab * LANES), BF16),
        scratch_shapes=[pltpu.VMEM((2, sum(s[2] for s in segs), R), F32)],
        compiler_params=_cparams(("parallel", "parallel", "parallel")), name=name,
    )(*args)


_OWN = lambda i, nq: i
_PREV = lambda i, nq: jnp.maximum(i - 1, 0)
_NEXT = lambda i, nq: jnp.minimum(i + 1, nq - 1)
_FIRST = lambda i, nq: 0


def _post_kernel(x_ref, o_ref, mod_ref, g_ref, b_ref, wo_ref, w1_ref, w2_ref, out_ref, *, cf, rows):
    subs = [slice(r, r + rows) for r in range(0, x_ref.shape[1], rows)]
    ys = [_dot(o_ref[0, s, :], wo_ref[...]) for s in subs]
    x1s = [_layer_norm(ALPHA * x_ref[0, s, :] + mod_ref[0, 2:3, :] * y, g_ref[0:1, :], b_ref[0:1, :])
           for s, y in zip(subs, ys)]
    hs = [(x1 * (1.0 + mod_ref[0, 4:5, :]) + mod_ref[0, 3:4, :]).astype(BF16) for x1 in x1s]
    accs = []
    for h in hs:
        acc = jnp.zeros(h.shape, F32)
        for c in range(0, D_FF, cf):
            u = jnp.maximum(_dot(h, w1_ref[:, c:c + cf]), 0.0)
            acc = acc + _dot((u * u).astype(BF16), w2_ref[c:c + cf, :])
        accs.append(acc)
    for s, x1, acc in zip(subs, x1s, accs):
        out_ref[0, s, :] = _layer_norm(ALPHA * x1 + mod_ref[0, 5:6, :] * acc, g_ref[1:2, :], b_ref[1:2, :])


def _post(x, o, mod, row_of, ln_g, ln_b, wo, w1, w2, tm=512, shared_mod=False, name="post"):
    shape = x.shape
    if shared_mod and shape[1] < tm and (shape[0] * shape[1]) % tm == 0:
        x, o = x.reshape(-1, tm, D_MODEL), o.reshape(-1, tm, D_MODEL)
    B, T, _ = x.shape
    tm = min(tm, T)
    const = lambda b, i: (0, 0)
    once = pl.Buffered(1)
    out = pl.pallas_call(
        functools.partial(_post_kernel, cf=1024, rows=min(tm, 256)),
        grid=(B, T // tm),
        in_specs=[pl.BlockSpec((1, tm, D_MODEL), lambda b, i: (b, i, 0)),
                  pl.BlockSpec((1, tm, D_MODEL), lambda b, i: (b, i, 0)),
                  pl.BlockSpec((1, 6, D_MODEL), lambda b, i: (row_of(b), 0, 0)),
                  pl.BlockSpec((2, D_MODEL), const), pl.BlockSpec((2, D_MODEL), const),
                  pl.BlockSpec(wo.shape, const, pipeline_mode=once),
                  pl.BlockSpec(w1.shape, const, pipeline_mode=once),
                  pl.BlockSpec(w2.shape, const, pipeline_mode=once)],
        out_specs=pl.BlockSpec((1, tm, D_MODEL), lambda b, i: (b, i, 0)),
        out_shape=jax.ShapeDtypeStruct(x.shape, F32),
        compiler_params=_cparams(("parallel", "parallel")), name=name,
    )(x, o, mod, ln_g, ln_b, wo, w1, w2)
    return out.reshape(shape)


def _rope_tables(T, R, lead, reps):
    n = R // 4
    t = jnp.arange(T)
    inv = ROPE_BASE ** (-jnp.arange(n, dtype=jnp.float32) / n)

    def cs(p):
        ang = p.astype(jnp.float32)[:, None] * inv
        return jnp.cos(ang), jnp.sin(ang)

    cr, sr = cs(t // GRID_W)
    cc, sc = cs(t % GRID_W)
    z = jnp.zeros_like(sr)
    cos = jnp.concatenate([cr, cr, cc, cc], -1)
    sup = jnp.concatenate([-sr, z, -sc, z], -1)
    sdn = jnp.concatenate([z, sr, z, sc], -1)
    tail = LANES - lead - reps * R

    def slab(a, fill):
        parts = [jnp.full((T, lead), fill, F32)] if lead else []
        parts += [a] * reps
        if tail:
            parts.append(jnp.full((T, tail), fill, F32))
        return jnp.concatenate(parts, -1)

    return slab(cos, 1.0), slab(sup, 0.0), slab(sdn, 0.0)


def _na_bias_blocks(rpb, rows):
    qr, kb_n = NA_QROWS, 3 * NA_QROWS
    n_dr, n_dc = 2 * NA_KH - 1, 2 * NA_KW - 1
    c = np.arange(GRID_W)
    c0 = np.clip(c - NA_KW // 2, 0, GRID_W - NA_KW)
    kc = np.arange(GRID_W)[:, None]
    col_ok = (kc >= c0[None]) & (kc < c0[None] + NA_KW)
    sel_c = ((np.arange(n_dc)[:, None, None] == (kc - c[None] + NA_KW - 1)[None]) & col_ok[None])
    H = rpb.shape[0]
    rpb2 = (rpb.astype(F32) * LOG2E).reshape(H // 2, 2, n_dr, n_dc)
    u = jnp.einsum("gxde,ejc->gxdjc", rpb2, jnp.asarray(sel_c, F32), precision=lax.Precision.HIGHEST)
    u = u + jnp.asarray(np.where(col_ok, 0.0, NEG_INF).astype(np.float32))
    blocks = jnp.concatenate([u[:, :, 1:], u[:, :, :-1]], axis=-1)
    rowok = np.zeros((3, kb_n, qr), np.float32)
    for v, r_base in enumerate((0, qr, rows - qr)):
        r = r_base + np.arange(qr)[None]
        kr = r_base - qr + np.arange(kb_n)[:, None]
        r0 = np.clip(r - NA_KH // 2, 0, rows - NA_KH)
        rowok[v] = np.where((kr >= r0) & (kr < r0 + NA_KH), 0.0, NEG_INF)
    rowok = np.repeat(rowok.reshape(3, kb_n * qr // 2, 2), GRID_W, axis=-1)
    return blocks, jnp.asarray(rowok)


def _swa_mask_tables(nb, reps):
    qi = np.arange(SWA_QB)[None, :]
    kj = np.arange(3 * SWA_QB)[:, None]
    tabs = []
    for blk in (0, 1, nb - 1):
        kpos = (blk - 1) * SWA_QB + kj
        ok = (np.abs(kj - SWA_QB - qi) <= SWA_WINDOW) & (kpos >= 0) & (kpos < nb * SWA_QB)
        tabs.append(np.tile(np.where(ok, 0.0, NEG_INF).astype(np.float32), (1, reps))[None])
    return jnp.asarray(np.stack(tabs, 0))


_SWA_PERM = np.array([8 * p + 4 * half + s for p in range(2) for s in range(4) for half in range(2)])


def _perm_heads(w, axis, dh=HALF):
    shape = w.shape
    w = w.reshape(shape[:axis] + (SWA_HEADS, dh) + shape[axis + 1:])
    w = jnp.take(w, jnp.asarray(_SWA_PERM), axis=axis)
    return w.reshape(shape)


def kernel(x_prompt, x_sample, cache_l0_k, cache_l0_v, cache_l1_k, cache_l1_v, cache_l2_ckv, cache_l2_kpe, cache_l3_k, cache_l3_v, c, c_ctx, w_mod, b_mod, ln_g, ln_b, w_mlp1, w_mlp2, l0_w_qkv, l0_lam, l0_subln, l0_w_o, l1_w_qkv, l1_rpb, l1_w_o, l2_w_a, l2_q_norm, l2_kv_norm, l2_w_uq, l2_w_ukv, l2_w_o, l3_w_qkv, l3_sink, l3_w_o):
    D = D_MODEL
    Bc, S, _ = x_prompt.shape
    Bl, T, _ = x_sample.shape
    P = cache_l0_k.shape[1]
    assert 1 + Bl <= MOD_ROWS and T % GRID_W == 0
    rows = T // GRID_W

    cond = jnp.concatenate([c_ctx[None], c, jnp.zeros((MOD_ROWS - 1 - Bl, D), F32)], 0)
    mod = _modulation(cond, w_mod, b_mod)
    ctx_row = lambda b: 0
    lat_row = lambda b: b + 1

    bf = lambda a: a.astype(BF16)
    tr = lambda a: jnp.swapaxes(a, -1, -2)
    w1 = bf(w_mlp1)
    w2 = bf(w_mlp2)
    qmul = HALF ** -0.5 * LOG2E

    tabs64 = _rope_tables(T, DIFF_DH, 0, 2)
    rope64 = (DIFF_DH // 4, tabs64)
    rope64_t = (DIFF_DH // 4, tuple(tr(t) for t in tabs64))

    def qkv_t(x, mod_l, row_of, w_k, w_t, nq, nk, lat, rope, name):
        if lat:
            return _project(x, mod_l, row_of, w_k, w_t, [(0, nk, rope, True, False, None)],
                            [(0, nq, rope, True, False, qmul), (nq, nk, False, True, False, None)],
                            rope=rope64 if rope else None, rope_t=rope64_t if rope else None, name=name)
        return _project(x, mod_l, row_of, w_k, w_t,
                        [(0, nk, False, True, True, None), (nk, nk, False, False, True, None)],
                        [(0, nq, False, True, False, qmul), (nq, nk, False, True, False, None)], name=name)

    def split_w(w, nq, nk):
        return bf(w[:, nq:]), bf(tr(jnp.concatenate([w[:, :nq], w[:, nq + nk:]], 1)))

    w_k, w_t = split_w(l0_w_qkv, D, D)
    wo = bf(l0_w_o)
    lam = l0_lam.astype(F32)
    subln = l0_subln.reshape(2 * DIFF_DH, 1)
    k, k0, v0, qt, vt = qkv_t(x_prompt, mod[0], ctx_row, w_k, w_t, D, D, False, False, "l0_proj_ctx")
    o = _flash(qt, k, vt, mode="diff", extra=(lam, subln), gpb=DIFF_HEADS, name="l0_attn_ctx")
    xp = _post(x_prompt, o, mod[0], ctx_row, ln_g[0], ln_b[0], wo, w1[0], w2[0], shared_mod=True, name="l0_post_ctx")
    k, qt, vt = qkv_t(x_sample, mod[0], lat_row, w_k, w_t, D, D, True, True, "l0_proj_lat")
    cache = (bf(cache_l0_k.reshape(Bl, P, D)), bf(tr(cache_l0_v.reshape(Bl, P, D))))
    o = _flash(qt, k, vt, cache, mode="diff", extra=(lam, subln), gpb=2, tq=512, name="l0_attn_lat")
    xs = _post(x_sample, o, mod[0], lat_row, ln_g[0], ln_b[0], wo, w1[0], w2[0], name="l0_post_lat")

    w_k, w_t = split_w(l1_w_qkv, D, D)
    wo = bf(l1_w_o)
    k, k1, v1, qt, vt = qkv_t(xp, mod[1], ctx_row, w_k, w_t, D, D, False, False, "l1_proj_ctx")
    o = _band(qt, [(k, vt, S, _OWN, None)], n_slab=1, tq=S, gpb=NA_HEADS // 2, name="l1_attn_ctx")
    xp = _post(xp, o, mod[1], ctx_row, ln_g[1], ln_b[1], wo, w1[1], w2[1], shared_mod=True, name="l1_post_ctx")
    k, qt, vt = qkv_t(xs, mod[1], lat_row, w_k, w_t, D, D, True, False, "l1_proj_lat")
    kc, vct = bf(cache_l1_k.reshape(Bl, P, D)), bf(tr(cache_l1_v.reshape(Bl, P, D)))
    tqn = NA_QROWS * GRID_W
    na_bias = _na_bias_blocks(l1_rpb, rows)
    o = _band(qt, [(k, vt, tqn, _PREV, 0), (k, vt, tqn, _OWN, tqn), (k, vt, tqn, _NEXT, 2 * tqn),
                   (kc, vct, P, _FIRST, None)], n_slab=1, tq=tqn, gpb=NA_HEADS // 2, na_bias=na_bias, name="l1_attn_lat")
    xs = _post(xs, o, mod[1], lat_row, ln_g[1], ln_b[1], wo, w1[1], w2[1], name="l1_post_lat")

    H, dq = MLA_HEADS, MLA_NOPE + MLA_ROPE
    wa = bf(jnp.pad(l2_w_a, ((0, 0), (0, 7 * LANES - l2_w_a.shape[1]))))
    wuq = jnp.pad(l2_w_uq.reshape(MLA_Q_RANK, H, dq), ((0, 0), (0, 0), (0, LANES - dq))).reshape(MLA_Q_RANK, H * LANES)
    wuqt = bf(tr(wuq))
    wukv = l2_w_ukv.reshape(MLA_KV_RANK, H, MLA_NOPE + MLA_VDIM)
    wk_n = jnp.pad(wukv[:, :, :MLA_NOPE], ((0, 0), (0, 0), (0, LANES - MLA_NOPE))).reshape(MLA_KV_RANK, H * LANES)
    place = np.zeros((LANES, H, LANES), np.float32)
    place[np.arange(MLA_ROPE), :, MLA_NOPE + np.arange(MLA_ROPE)] = 1.0
    wk = bf(jnp.concatenate([wk_n, jnp.asarray(place.reshape(LANES, H * LANES))], 0))
    wvt = bf(tr(wukv[:, :, MLA_NOPE:].reshape(MLA_KV_RANK, H * MLA_VDIM)))
    wo = bf(l2_w_o)
    qn, kvn = l2_q_norm.reshape(1, -1), l2_kv_norm.reshape(1, -1)
    mscale = dq ** -0.5 * LOG2E
    qt, k, vt, ckv2, kpe2 = _mla_project(xp, mod[2], ctx_row, wa, qn, kvn, wuqt, wk, wvt, mscale, name="l2_proj_ctx")
    o = _flash(qt, k, vt, mode="mla", gpb=MLA_HEADS // 2, name="l2_attn_ctx")
    xp = _post(xp, o, mod[2], ctx_row, ln_g[2], ln_b[2], wo, w1[2], w2[2], shared_mod=True, name="l2_post_ctx")
    tabs = tuple(tr(t) for t in _rope_tables(T, MLA_ROPE, MLA_NOPE, 1)) + _rope_tables(T, MLA_ROPE, 0, 1)
    qt, k, vt = _mla_project(xs, mod[2], lat_row, wa, qn, kvn, wuqt, wk, wvt, mscale, rope=tabs, name="l2_proj_lat")
    kpe_slab = jnp.pad(cache_l2_kpe, ((0, 0), (0, 0), (0, LANES - MLA_ROPE)))
    cache = _mla_cache_expand(cache_l2_ckv, kpe_slab, wk, wvt)
    o = _flash(qt, k, vt, cache, mode="mla", tk=256, name="l2_attn_lat")
    xs = _post(xs, o, mod[2], lat_row, ln_g[2], ln_b[2], wo, w1[2], w2[2], name="l2_post_lat")

    nq3, nk3 = SWA_HEADS * HALF, SWA_KV_HEADS * HALF
    w = jnp.concatenate([_perm_heads(l3_w_qkv[:, :nq3], 1), l3_w_qkv[:, nq3:]], 1)
    w_k, w_t = split_w(w, nq3, nk3)
    wo = bf(_perm_heads(l3_w_o, 0))
    sink = jnp.take(l3_sink.astype(F32), jnp.asarray(_SWA_PERM)) * LOG2E

    def sink_rows(tq):
        return jnp.repeat(sink.reshape(2, 1, 8), tq, axis=-1)

    k, k3, v3, qt, vt = qkv_t(xp, mod[3], ctx_row, w_k, w_t, nq3, nk3, False, False, "l3_proj_ctx")
    o = _band(qt, [(k, vt, S, _OWN, None)], n_slab=4, tq=S, gpb=2, sink=sink_rows(S), name="l3_attn_ctx")
    xp = _post(xp, o, mod[3], ctx_row, ln_g[3], ln_b[3], wo, w1[3], w2[3], shared_mod=True, name="l3_post_ctx")
    k, qt, vt = qkv_t(xs, mod[3], lat_row, w_k, w_t, nq3, nk3, True, True, "l3_proj_lat")
    kc, vct = bf(cache_l3_k.reshape(Bl, P, nk3)), bf(tr(cache_l3_v.reshape(Bl, P, nk3)))
    mask = _swa_mask_tables(T // SWA_QB, 8)
    o = _band(qt, [(k, vt, SWA_QB, _PREV, 0), (k, vt, SWA_QB, _OWN, SWA_QB), (k, vt, SWA_QB, _NEXT, 2 * SWA_QB),
                   (kc, vct, P, _FIRST, None)], n_slab=4, tq=SWA_QB, gpb=2, bias=mask, sink=sink_rows(SWA_QB),
              name="l3_attn_lat")
    xs = _post(xs, o, mod[3], lat_row, ln_g[3], ln_b[3], wo, w1[3], w2[3], name="l3_post_lat")

    return (xp, xs,
            k0.reshape(Bc, S, DIFF_HEADS, 2 * DIFF_DH), v0.reshape(Bc, S, DIFF_HEADS, 2 * DIFF_DH),
            k1.reshape(Bc, S, NA_HEADS, HALF), v1.reshape(Bc, S, NA_HEADS, HALF),
            ckv2, kpe2,
            k3.reshape(Bc, S, SWA_KV_HEADS, HALF), v3.reshape(Bc, S, SWA_KV_HEADS, HALF))
```

```python
import functools
import math

import numpy as np
import jax
import jax.numpy as jnp
from jax import lax
from jax.experimental import pallas as pl
from jax.experimental.pallas import tpu as pltpu

D_MODEL = 1024
DEPTH = 4
GRID_W = 64
ROPE_BASE = 10000.0
LN_EPS = 1e-5
NEG_INF = -1e30
ALPHA = (2 * DEPTH) ** 0.25
D_FF = 4 * D_MODEL
DIFF_DH = 64
DIFF_HEADS = 8
DIFF_LAMBDA_INIT = 0.8 - 0.6 * math.exp(-0.3 * 0)
NA_HEADS = 16
NA_KH = 8
NA_KW = 16
NA_QROWS = 4
MLA_HEADS = 16
MLA_NOPE = 64
MLA_ROPE = 32
MLA_VDIM = 64
MLA_Q_RANK = 512
MLA_KV_RANK = 256
SWA_HEADS = 16
SWA_KV_HEADS = 4
SWA_WINDOW = 128
SWA_QB = 128
LOG2E = math.log2(math.e)

LANES = 128
HALF = LANES // 2
BF16_ROWS = 16
MOD_ROWS = 16
VMEM_LIMIT = 56 * 1024 * 1024

MOD_COLS = 2048
PROJ_ROWS, PROJ_COLS = 512, 256
MLA_PROJ_COLS = 512
POST_ROWS, POST_SUB_ROWS, POST_FF_COLS = 512, 256, 1024
FLASH_QUERIES, FLASH_KEYS = 1024, 512
DIFF_FLASH = dict(gpb=2, tq=512, tk=512)
MLA_FLASH = dict(gpb=1, tq=1024, tk=256)

F32 = jnp.float32
BF16 = jnp.bfloat16


def _cparams(sem):
    return pltpu.CompilerParams(dimension_semantics=sem, vmem_limit_bytes=VMEM_LIMIT)


def _dot(a, b):
    return jnp.dot(a, b, preferred_element_type=F32)


def _layer_norm(x, g, b):
    mu = jnp.mean(x, -1, keepdims=True)
    xc = x - mu
    var = jnp.mean(xc * xc, -1, keepdims=True)
    return xc * lax.rsqrt(var + LN_EPS) * g + b


def _rms_norm(x, g, axis=-1):
    return x * lax.rsqrt(jnp.mean(x * x, axis, keepdims=True) + LN_EPS) * g


def _rope_slab(x, cos, sin_up, sin_dn, n):
    return x * cos + pltpu.roll(x, LANES - n, 1) * sin_up + pltpu.roll(x, n, 1) * sin_dn


def _rope_slab_t(x, cos, sin_up, sin_dn, n):
    up = jnp.concatenate([x[n:], x[:n]], axis=0)
    dn = jnp.concatenate([x[LANES - n:], x[:LANES - n]], axis=0)
    return x * cos + up * sin_up + dn * sin_dn


def _split_heads_t(qt):
    lo = lax.broadcasted_iota(jnp.int32, qt.shape, 0) < HALF
    zero = jnp.zeros_like(qt)
    return jnp.concatenate([jnp.where(lo, qt, zero), jnp.where(lo, zero, qt)], axis=1)


def _with_ones_row(vt):
    r = lax.broadcasted_iota(jnp.int32, (BF16_ROWS, vt.shape[1]), 0)
    return jnp.concatenate([vt, jnp.where(r == 0, 1.0, 0.0).astype(BF16)], axis=0)


def _modulated(x_ref, mod_ref):
    return x_ref[0] * (1.0 + mod_ref[0, 1:2, :]) + mod_ref[0, 0:1, :]


def _mod_kernel(c_ref, w_ref, b_ref, o_ref):
    c = c_ref[...]
    s = c / (1.0 + jnp.exp(-c))
    o_ref[0] = _dot(s.astype(BF16), w_ref[0].astype(BF16)) + b_ref[0]


def _modulation(cond, w_mod, b_mod):
    tn = MOD_COLS
    n = 6 * D_MODEL
    out = pl.pallas_call(
        _mod_kernel,
        grid=(DEPTH, n // tn),
        in_specs=[pl.BlockSpec((MOD_ROWS, D_MODEL), lambda l, j: (0, 0)),
                  pl.BlockSpec((1, D_MODEL, tn), lambda l, j: (l, 0, j)),
                  pl.BlockSpec((1, 1, tn), lambda l, j: (l, 0, j))],
        out_specs=pl.BlockSpec((1, MOD_ROWS, tn), lambda l, j: (l, 0, j)),
        out_shape=jax.ShapeDtypeStruct((DEPTH, MOD_ROWS, n), F32),
        compiler_params=_cparams(("parallel", "parallel")),
        name="modulation",
    )(cond, w_mod, b_mod.reshape(DEPTH, 1, n))
    return out.reshape(DEPTH, MOD_ROWS, 6, D_MODEL)


def _proj_kernel(*refs, segs, tsegs, rope_n, cw):
    x_ref, mod_ref = refs[:2]
    pos = 2
    w_ref = wt_ref = None
    if segs:
        w_ref = refs[pos]
        pos += 1
    if tsegs:
        wt_ref = refs[pos]
        pos += 1
    if rope_n and any(s[2] for s in segs):
        cos, sup, sdn = (r[...] for r in refs[pos:pos + 3])
        pos += 3
    if rope_n and any(s[2] for s in tsegs):
        cos_t, sup_t, sdn_t = (r[...] for r in refs[pos:pos + 3])
        pos += 3
    outs = list(refs[pos:])
    h32 = _modulated(x_ref, mod_ref)
    h = h32.astype(BF16)
    for (start, width, rope, want16, want32, mul) in segs:
        o16 = outs.pop(0) if want16 else None
        o32 = outs.pop(0) if want32 else None
        for c in range(0, width, cw):
            acc = _dot(h, w_ref[:, start + c:start + c + cw])
            if want32:
                o32[0, :, c:c + cw] = acc
            if mul is not None:
                acc = acc * mul
            if rope:
                acc = jnp.concatenate(
                    [_rope_slab(acc[:, s:s + LANES], cos, sup, sdn, rope_n) for s in range(0, cw, LANES)], axis=1)
            if want16:
                o16[0, :, c:c + cw] = acc.astype(BF16)
    if tsegs:
        ht = h32.T.astype(BF16)
        for (start, width, rope, _, _, mul) in tsegs:
            o16 = outs.pop(0)
            for c in range(0, width, cw):
                acc = _dot(wt_ref[start + c:start + c + cw, :], ht)
                if mul is not None:
                    acc = acc * mul
                if rope:
                    acc = jnp.concatenate(
                        [_rope_slab_t(acc[s:s + LANES], cos_t, sup_t, sdn_t, rope_n) for s in range(0, cw, LANES)],
                        axis=0)
                o16[0, c:c + cw, :] = acc.astype(BF16)


def _project(x, mod, row_of, w, wt, segs, tsegs, rope=None, rope_t=None, tm=PROJ_ROWS, cw=PROJ_COLS, name="proj"):
    B, T, _ = x.shape
    tm = min(tm, T)
    cw = min([cw] + [s[1] for s in list(segs) + list(tsegs)])
    const = lambda b, i: (0, 0)
    in_specs = [pl.BlockSpec((1, tm, D_MODEL), lambda b, i: (b, i, 0)),
                pl.BlockSpec((1, 6, D_MODEL), lambda b, i: (row_of(b), 0, 0))]
    args = [x, mod]
    if segs:
        in_specs.append(pl.BlockSpec(w.shape, const))
        args.append(w)
    if tsegs:
        in_specs.append(pl.BlockSpec(wt.shape, const))
        args.append(wt)
    rope_n = 0
    if rope is not None:
        rope_n, tabs = rope
        in_specs += [pl.BlockSpec((tm, LANES), lambda b, i: (i, 0))] * 3
        args += list(tabs)
    if rope_t is not None:
        rope_n, tabs = rope_t
        in_specs += [pl.BlockSpec((LANES, tm), lambda b, i: (0, i))] * 3
        args += list(tabs)
    out_specs, out_shape = [], []
    for (_, width, _, want16, want32, _) in segs:
        for want, dt in ((want16, BF16), (want32, F32)):
            if want:
                out_specs.append(pl.BlockSpec((1, tm, width), lambda b, i: (b, i, 0)))
                out_shape.append(jax.ShapeDtypeStruct((B, T, width), dt))
    for (_, width, _, _, _, _) in tsegs:
        out_specs.append(pl.BlockSpec((1, width, tm), lambda b, i: (b, 0, i)))
        out_shape.append(jax.ShapeDtypeStruct((B, width, T), BF16))
    return pl.pallas_call(
        functools.partial(_proj_kernel, segs=tuple(segs), tsegs=tuple(tsegs), rope_n=rope_n, cw=cw),
        grid=(B, T // tm), in_specs=in_specs, out_specs=out_specs, out_shape=out_shape,
        compiler_params=_cparams(("parallel", "parallel")), name=name,
    )(*args)


def _mla_proj_kernel(*refs, lat, scale, cw):
    x_ref, mod_ref, wa_ref, qn_ref, kvn_ref, wuqt_ref, wk_ref, wvt_ref = refs[:8]
    pos = 8
    if lat:
        qcos, qsup, qsdn, kcos, ksup, ksdn = (r[...] for r in refs[8:14])
        pos = 14
    qt_out, k_out, vt_out = refs[pos:pos + 3]
    h = _modulated(x_ref, mod_ref).astype(BF16)
    a = _dot(h, wa_ref[...])
    cq = _rms_norm(a[:, :MLA_Q_RANK], qn_ref[...])
    ckv = _rms_norm(a[:, MLA_Q_RANK:MLA_Q_RANK + MLA_KV_RANK], kvn_ref[...])
    kpe = a[:, MLA_Q_RANK + MLA_KV_RANK:]
    if lat:
        kpe = _rope_slab(kpe, kcos, ksup, ksdn, MLA_ROPE // 4)
    else:
        ckv_state, kpe_state = refs[pos + 3:pos + 5]
        ckv_state[0] = ckv
        kpe_state[0] = kpe[:, :MLA_ROPE]
    cqt = cq.T.astype(BF16)
    n_q = wuqt_ref.shape[0]
    for c in range(0, n_q, cw):
        q = _dot(wuqt_ref[c:c + cw, :], cqt) * scale
        if lat:
            q = jnp.concatenate(
                [_rope_slab_t(q[s:s + LANES], qcos, qsup, qsdn, MLA_ROPE // 4) for s in range(0, cw, LANES)], axis=0)
        qt_out[0, c:c + cw, :] = q.astype(BF16)
    kin = jnp.concatenate([ckv.astype(BF16), kpe.astype(BF16)], axis=1)
    for c in range(0, wk_ref.shape[1], cw):
        k_out[0, :, c:c + cw] = _dot(kin, wk_ref[:, c:c + cw]).astype(BF16)
    ckvt = ckv.T.astype(BF16)
    for c in range(0, wvt_ref.shape[0], cw):
        vt_out[0, c:c + cw, :] = _dot(wvt_ref[c:c + cw, :], ckvt).astype(BF16)


def _mla_project(x, mod, row_of, wa, qn, kvn, wuqt, wk, wvt, scale, rope=None, tm=PROJ_ROWS, name="mla_proj"):
    B, T, _ = x.shape
    tm = min(tm, T)
    lat = rope is not None
    const = lambda b, i: (0, 0)
    in_specs = [pl.BlockSpec((1, tm, D_MODEL), lambda b, i: (b, i, 0)),
                pl.BlockSpec((1, 6, D_MODEL), lambda b, i: (row_of(b), 0, 0))]
    in_specs += [pl.BlockSpec(a.shape, const) for a in (wa, qn, kvn, wuqt, wk, wvt)]
    args = [x, mod, wa, qn, kvn, wuqt, wk, wvt]
    if lat:
        in_specs += [pl.BlockSpec((LANES, tm), lambda b, i: (0, i))] * 3
        in_specs += [pl.BlockSpec((tm, LANES), lambda b, i: (i, 0))] * 3
        args += list(rope)
    tok = lambda b, i: (b, i, 0)
    feat = lambda b, i: (b, 0, i)
    out_specs = [pl.BlockSpec((1, wuqt.shape[0], tm), feat), pl.BlockSpec((1, tm, wk.shape[1]), tok),
                 pl.BlockSpec((1, wvt.shape[0], tm), feat)]
    out_shape = [jax.ShapeDtypeStruct((B, wuqt.shape[0], T), BF16), jax.ShapeDtypeStruct((B, T, wk.shape[1]), BF16),
                 jax.ShapeDtypeStruct((B, wvt.shape[0], T), BF16)]
    if not lat:
        out_specs += [pl.BlockSpec((1, tm, MLA_KV_RANK), tok), pl.BlockSpec((1, tm, MLA_ROPE), tok)]
        out_shape += [jax.ShapeDtypeStruct((B, T, MLA_KV_RANK), F32), jax.ShapeDtypeStruct((B, T, MLA_ROPE), F32)]
    return pl.pallas_call(
        functools.partial(_mla_proj_kernel, lat=lat, scale=scale, cw=MLA_PROJ_COLS),
        grid=(B, T // tm), in_specs=in_specs, out_specs=out_specs, out_shape=out_shape,
        compiler_params=_cparams(("parallel", "parallel")), name=name,
    )(*args)


def _mla_cache_kernel(ckv_ref, kpe_ref, wk_ref, wvt_ref, k_out, vt_out):
    ckv = ckv_ref[0]
    kin = jnp.concatenate([ckv.astype(BF16), kpe_ref[0].astype(BF16)], axis=1)
    k_out[0] = _dot(kin, wk_ref[...]).astype(BF16)
    vt_out[0] = _dot(wvt_ref[...], ckv.T.astype(BF16)).astype(BF16)


def _mla_cache_expand(c_ckv, c_kpe_slab, wk, wvt):
    B, P, _ = c_ckv.shape
    const = lambda b: (0, 0)
    return pl.pallas_call(
        _mla_cache_kernel, grid=(B,),
        in_specs=[pl.BlockSpec((1, P, MLA_KV_RANK), lambda b: (b, 0, 0)),
                  pl.BlockSpec((1, P, LANES), lambda b: (b, 0, 0)),
                  pl.BlockSpec(wk.shape, const), pl.BlockSpec(wvt.shape, const)],
        out_specs=[pl.BlockSpec((1, P, wk.shape[1]), lambda b: (b, 0, 0)),
                   pl.BlockSpec((1, wvt.shape[0], P), lambda b: (b, 0, 0))],
        out_shape=[jax.ShapeDtypeStruct((B, P, wk.shape[1]), BF16),
                   jax.ShapeDtypeStruct((B, wvt.shape[0], P), BF16)],
        compiler_params=_cparams(("parallel",)), name="mla_cache_expand",
    )(c_ckv, c_kpe_slab, wk, wvt)


def _flash_kernel(*refs, mode, has_cache, gpb, tq, tk):
    qt_ref = refs[0]
    pos = 1
    if has_cache:
        kc_ref, vct_ref = refs[1:3]
        pos = 3
    k_ref, vt_ref = refs[pos:pos + 2]
    pos += 2
    if mode == "diff":
        lam_ref, subln_ref = refs[pos:pos + 2]
        pos += 2
    o_ref, s_sc, acc_sc = refs[pos:pos + 3]
    dv = LANES if mode == "diff" else HALF
    wq = LANES if mode == "diff" else 2 * LANES

    blocks = []
    if has_cache:
        pc = kc_ref.shape[1]
        blocks += [(kc_ref, vct_ref, j, min(tk, pc - j)) for j in range(0, pc, tk)]
    blocks += [(k_ref, vt_ref, j, tk) for j in range(0, k_ref.shape[1], tk)]

    qts = [qt_ref[0, g * wq:(g + 1) * wq, :] for g in range(gpb)]
    if mode == "diff":
        qts = [_split_heads_t(qt) for qt in qts]

    def logits(n, g):
        kr, _, j, nk = blocks[n]
        k = kr[0, j:j + nk, g * wq:(g + 1) * wq]
        buf = 2 * g + n % 2
        if mode == "diff":
            s_sc[buf, :nk, :] = _dot(k, qts[g])
        else:
            s_sc[buf, :nk, :tq] = _dot(k[:, :LANES], qts[g][:LANES])
            s_sc[buf, :nk, tq:] = _dot(k[:, LANES:], qts[g][LANES:])

    for g in range(gpb):
        logits(0, g)
    m = [None] * gpb
    for n in range(len(blocks)):
        for g in range(gpb):
            if n + 1 < len(blocks):
                logits(n + 1, g)
            _, vr, j, nk = blocks[n]
            vt = vr[0, g * LANES:(g + 1) * LANES, j:j + nk]
            s = s_sc[2 * g + n % 2, :nk, :]
            m_cur = jnp.max(s, 0, keepdims=True)
            m_new = m_cur if m[g] is None else jnp.maximum(m[g], m_cur)
            pb = jnp.exp2(s - m_new).astype(BF16)
            if mode == "diff":
                pv = _dot(_with_ones_row(vt), pb)
            else:
                pv = jnp.concatenate([_dot(_with_ones_row(vt[:HALF]), pb[:, :tq]),
                                      _dot(_with_ones_row(vt[HALF:]), pb[:, tq:])], axis=1)
            if m[g] is None:
                acc_sc[g] = pv
            else:
                acc_sc[g] = jnp.exp2(m[g] - m_new) * acc_sc[g] + pv
            m[g] = m_new

    if mode == "diff":
        lam = lam_ref[...]
        lam_full = (jnp.exp(jnp.sum(lam[0:1] * lam[1:2], keepdims=True))
                    - jnp.exp(jnp.sum(lam[2:3] * lam[3:4], keepdims=True)) + DIFF_LAMBDA_INIT)
    for g in range(gpb):
        acc = acc_sc[g]
        o = acc[:dv] / acc[dv:dv + 1]
        o0, o1 = o[:, :tq], o[:, tq:]
        if mode == "diff":
            out_t = _rms_norm(o0 - lam_full * o1, subln_ref[...], axis=0) * (1.0 - DIFF_LAMBDA_INIT)
        else:
            out_t = jnp.concatenate([o0, o1], axis=0)
        o_ref[0, :, g * LANES:(g + 1) * LANES] = out_t.T.astype(o_ref.dtype)


def _flash(qt, k, vt, cache=None, *, mode, extra=(), gpb=1, tq=FLASH_QUERIES, tk=FLASH_KEYS, name="flash"):
    B, _, T = qt.shape
    S = k.shape[1]
    wq = LANES if mode == "diff" else 2 * LANES
    G = vt.shape[1] // LANES
    tq, tk = min(tq, T), min(tk, S)
    in_specs = [pl.BlockSpec((1, gpb * wq, tq), lambda b, g, i: (b, g, i))]
    args = [qt]
    if cache is not None:
        kc, vct = cache
        P = kc.shape[1]
        in_specs += [pl.BlockSpec((1, P, gpb * wq), lambda b, g, i: (b, 0, g)),
                     pl.BlockSpec((1, gpb * LANES, P), lambda b, g, i: (b, g, 0))]
        args += [kc, vct]
    in_specs += [pl.BlockSpec((1, S, gpb * wq), lambda b, g, i: (b, 0, g)),
                 pl.BlockSpec((1, gpb * LANES, S), lambda b, g, i: (b, g, 0))]
    args += [k, vt]
    for e in extra:
        in_specs.append(pl.BlockSpec(e.shape, lambda b, g, i: (0, 0)))
        args.append(e)
    dv = LANES if mode == "diff" else HALF
    return pl.pallas_call(
        functools.partial(_flash_kernel, mode=mode, has_cache=cache is not None, gpb=gpb, tq=tq, tk=tk),
        grid=(B, G // gpb, T // tq), in_specs=in_specs,
        out_specs=pl.BlockSpec((1, tq, gpb * LANES), lambda b, g, i: (b, i, g)),
        out_shape=jax.ShapeDtypeStruct((B, T, G * LANES), BF16),
        scratch_shapes=[pltpu.VMEM((2 * gpb, tk, 2 * tq), F32), pltpu.VMEM((gpb, dv + BF16_ROWS, 2 * tq), F32)],
        compiler_params=_cparams(("parallel", "parallel", "parallel")), name=name,
    )(*args)


def _band_kernel(*refs, n_seg, seg_bias, n_slab, gpb, bias_kind, has_sink, tq):
    qt_ref = refs[0]
    kv = refs[1:1 + 2 * n_seg]
    pos = 1 + 2 * n_seg
    sink_ref = None
    if bias_kind == "table":
        bias_ref = refs[pos]
        pos += 1
    elif bias_kind == "na":
        blk_ref, rowok_ref = refs[pos:pos + 2]
        pos += 2
    if has_sink:
        sink_ref = refs[pos]
        pos += 1
    o_ref, s_sc = refs[pos:pos + 2]
    rows = [kv[2 * i].shape[1] for i in range(n_seg)]
    offs = [sum(rows[:i]) for i in range(n_seg)]

    def bias(g, off, n):
        if bias_kind == "table":
            return bias_ref[0, 0, off:off + n, :]
        tiles = []
        for kb in range(off // GRID_W, (off + n) // GRID_W):
            tiles.append(jnp.concatenate(
                [blk_ref[g, x, kb - 2 * p + 2] + rowok_ref[0, 2 * kb + p:2 * kb + p + 1, :]
                 for x in range(2) for p in range(NA_QROWS // 2)], axis=1))
        return jnp.concatenate(tiles, axis=0)

    def logits(g):
        q0 = g * n_slab * LANES
        qst = jnp.concatenate(
            [_split_heads_t(qt_ref[0, q0 + s * LANES:q0 + (s + 1) * LANES, :]) for s in range(n_slab)], axis=1)
        k = jnp.concatenate([kv[2 * i][0, :, g * LANES:(g + 1) * LANES] for i in range(n_seg)], axis=0)
        lg = _dot(k, qst)
        for i in range(n_seg):
            part = lg[offs[i]:offs[i] + rows[i]]
            if seg_bias[i] is not None:
                part = part + bias(g, seg_bias[i], rows[i])
            s_sc[g % 2, offs[i]:offs[i] + rows[i], :] = part

    logits(0)
    for g in range(gpb):
        if g + 1 < gpb:
            logits(g + 1)
        q0 = g * n_slab * LANES
        s = s_sc[g % 2]
        m = jnp.max(s, 0, keepdims=True)
        if has_sink:
            sink = sink_ref[g]
            m = jnp.maximum(m, sink)
        vt = jnp.concatenate([kv[2 * i + 1][0, g * LANES:(g + 1) * LANES, :] for i in range(n_seg)], axis=1)
        acc = _dot(_with_ones_row(vt), jnp.exp2(s - m).astype(BF16))
        l = acc[LANES:LANES + 1]
        if has_sink:
            l = l + jnp.exp2(sink - m)
        o = acc[:LANES] / l
        for s in range(n_slab):
            c = 2 * s * tq
            out_t = jnp.concatenate([o[:HALF, c:c + tq], o[HALF:, c + tq:c + 2 * tq]], axis=0)
            o_ref[0, :, q0 + s * LANES:q0 + (s + 1) * LANES] = out_t.T.astype(o_ref.dtype)


def _band(qt, segs, *, n_slab, tq, gpb=1, bias=None, na_bias=None, sink=None, name="band"):
    B, _, T = qt.shape
    G = qt.shape[1] // (n_slab * LANES)
    nq = T // tq
    R = n_slab * 2 * tq
    in_specs = [pl.BlockSpec((1, gpb * n_slab * LANES, tq), lambda g, b, i: (b, g, i))]
    args = [qt]
    seg_bias = []
    for (k, vt, ts, blk_of, boff) in segs:
        in_specs += [pl.BlockSpec((1, ts, gpb * LANES), (lambda f: lambda g, b, i: (b, f(i, nq), g))(blk_of)),
                     pl.BlockSpec((1, gpb * LANES, ts), (lambda f: lambda g, b, i: (b, g, f(i, nq)))(blk_of))]
        args += [k, vt]
        seg_bias.append(boff)
    var_of = lambda i: jnp.where(i == 0, 0, jnp.where(i == nq - 1, 2, 1))
    bias_kind = None
    if bias is not None:
        bias_kind = "table"
        in_specs.append(pl.BlockSpec((1, 1, bias.shape[2], R), lambda g, b, i: (var_of(i), 0, 0, 0)))
        args.append(bias)
    elif na_bias is not None:
        bias_kind = "na"
        blk, rowok = na_bias
        in_specs += [pl.BlockSpec((gpb,) + blk.shape[1:], lambda g, b, i: (g, 0, 0, 0, 0)),
                     pl.BlockSpec((1,) + rowok.shape[1:], lambda g, b, i: (var_of(i), 0, 0))]
        args += [blk, rowok]
    if sink is not None:
        in_specs.append(pl.BlockSpec((gpb, 1, R), lambda g, b, i: (g, 0, 0)))
        args.append(sink)
    return pl.pallas_call(
        functools.partial(_band_kernel, n_seg=len(segs), seg_bias=tuple(seg_bias), n_slab=n_slab, gpb=gpb,
                          bias_kind=bias_kind, has_sink=sink is not None, tq=tq),
        grid=(G // gpb, B, nq), in_specs=in_specs,
        out_specs=pl.BlockSpec((1, tq, gpb * n_slab * LANES), lambda g, b, i: (b, i, g)),
        out_shape=jax.ShapeDtypeStruct((B, T, G * n_slab * LANES), BF16),
        scratch_shapes=[pltpu.VMEM((2, sum(s[2] for s in segs), R), F32)],
        compiler_params=_cparams(("parallel", "parallel", "parallel")), name=name,
    )(*args)


_OWN = lambda i, nq: i
_PREV = lambda i, nq: jnp.maximum(i - 1, 0)
_NEXT = lambda i, nq: jnp.minimum(i + 1, nq - 1)
_FIRST = lambda i, nq: 0


def _post_kernel(x_ref, o_ref, mod_ref, g_ref, b_ref, wo_ref, w1_ref, w2_ref, out_ref, *, cf, rows):
    subs = [slice(r, r + rows) for r in range(0, x_ref.shape[1], rows)]
    ys = [_dot(o_ref[0, s, :], wo_ref[...]) for s in subs]
    x1s = [_layer_norm(ALPHA * x_ref[0, s, :] + mod_ref[0, 2:3, :] * y, g_ref[0:1, :], b_ref[0:1, :])
           for s, y in zip(subs, ys)]
    hs = [(x1 * (1.0 + mod_ref[0, 4:5, :]) + mod_ref[0, 3:4, :]).astype(BF16) for x1 in x1s]
    accs = []
    for h in hs:
        acc = jnp.zeros(h.shape, F32)
        for c in range(0, D_FF, cf):
            u = jnp.maximum(_dot(h, w1_ref[:, c:c + cf]), 0.0)
            acc = acc + _dot((u * u).astype(BF16), w2_ref[c:c + cf, :])
        accs.append(acc)
    for s, x1, acc in zip(subs, x1s, accs):
        out_ref[0, s, :] = _layer_norm(ALPHA * x1 + mod_ref[0, 5:6, :] * acc, g_ref[1:2, :], b_ref[1:2, :])


def _post(x, o, mod, row_of, ln_g, ln_b, wo, w1, w2, tm=POST_ROWS, shared_mod=False, name="post"):
    shape = x.shape
    if shared_mod and shape[1] < tm and (shape[0] * shape[1]) % tm == 0:
        x, o = x.reshape(-1, tm, D_MODEL), o.reshape(-1, tm, D_MODEL)
    B, T, _ = x.shape
    tm = min(tm, T)
    const = lambda b, i: (0, 0)
    once = pl.Buffered(1)
    out = pl.pallas_call(
        functools.partial(_post_kernel, cf=POST_FF_COLS, rows=min(tm, POST_SUB_ROWS)),
        grid=(B, T // tm),
        in_specs=[pl.BlockSpec((1, tm, D_MODEL), lambda b, i: (b, i, 0)),
                  pl.BlockSpec((1, tm, D_MODEL), lambda b, i: (b, i, 0)),
                  pl.BlockSpec((1, 6, D_MODEL), lambda b, i: (row_of(b), 0, 0)),
                  pl.BlockSpec((2, D_MODEL), const), pl.BlockSpec((2, D_MODEL), const),
                  pl.BlockSpec(wo.shape, const, pipeline_mode=once),
                  pl.BlockSpec(w1.shape, const, pipeline_mode=once),
                  pl.BlockSpec(w2.shape, const, pipeline_mode=once)],
        out_specs=pl.BlockSpec((1, tm, D_MODEL), lambda b, i: (b, i, 0)),
        out_shape=jax.ShapeDtypeStruct(x.shape, F32),
        compiler_params=_cparams(("parallel", "parallel")), name=name,
    )(x, o, mod, ln_g, ln_b, wo, w1, w2)
    return out.reshape(shape)


def _rope_tables(T, R, lead, reps):
    n = R // 4
    t = jnp.arange(T)
    inv = ROPE_BASE ** (-jnp.arange(n, dtype=jnp.float32) / n)

    def cs(p):
        ang = p.astype(jnp.float32)[:, None] * inv
        return jnp.cos(ang), jnp.sin(ang)

    cr, sr = cs(t // GRID_W)
    cc, sc = cs(t % GRID_W)
    z = jnp.zeros_like(sr)
    cos = jnp.concatenate([cr, cr, cc, cc], -1)
    sup = jnp.concatenate([-sr, z, -sc, z], -1)
    sdn = jnp.concatenate([z, sr, z, sc], -1)
    tail = LANES - lead - reps * R

    def slab(a, fill):
        parts = [jnp.full((T, lead), fill, F32)] if lead else []
        parts += [a] * reps
        if tail:
            parts.append(jnp.full((T, tail), fill, F32))
        return jnp.concatenate(parts, -1)

    return slab(cos, 1.0), slab(sup, 0.0), slab(sdn, 0.0)


def _na_bias_blocks(rpb, rows):
    qr, kb_n = NA_QROWS, 3 * NA_QROWS
    n_dr, n_dc = 2 * NA_KH - 1, 2 * NA_KW - 1
    c = np.arange(GRID_W)
    c0 = np.clip(c - NA_KW // 2, 0, GRID_W - NA_KW)
    kc = np.arange(GRID_W)[:, None]
    col_ok = (kc >= c0[None]) & (kc < c0[None] + NA_KW)
    sel_c = ((np.arange(n_dc)[:, None, None] == (kc - c[None] + NA_KW - 1)[None]) & col_ok[None])
    H = rpb.shape[0]
    rpb2 = (rpb.astype(F32) * LOG2E).reshape(H // 2, 2, n_dr, n_dc)
    u = jnp.einsum("gxde,ejc->gxdjc", rpb2, jnp.asarray(sel_c, F32), precision=lax.Precision.HIGHEST)
    u = u + jnp.asarray(np.where(col_ok, 0.0, NEG_INF).astype(np.float32))
    blocks = jnp.concatenate([u[:, :, 1:], u[:, :, :-1]], axis=-1)
    rowok = np.zeros((3, kb_n, qr), np.float32)
    for v, r_base in enumerate((0, qr, rows - qr)):
        r = r_base + np.arange(qr)[None]
        kr = r_base - qr + np.arange(kb_n)[:, None]
        r0 = np.clip(r - NA_KH // 2, 0, rows - NA_KH)
        rowok[v] = np.where((kr >= r0) & (kr < r0 + NA_KH), 0.0, NEG_INF)
    rowok = np.repeat(rowok.reshape(3, kb_n * qr // 2, 2), GRID_W, axis=-1)
    return blocks, jnp.asarray(rowok)


def _swa_mask_tables(nb, reps):
    qi = np.arange(SWA_QB)[None, :]
    kj = np.arange(3 * SWA_QB)[:, None]
    tabs = []
    for blk in (0, 1, nb - 1):
        kpos = (blk - 1) * SWA_QB + kj
        ok = (np.abs(kj - SWA_QB - qi) <= SWA_WINDOW) & (kpos >= 0) & (kpos < nb * SWA_QB)
        tabs.append(np.tile(np.where(ok, 0.0, NEG_INF).astype(np.float32), (1, reps))[None])
    return jnp.asarray(np.stack(tabs, 0))


_SWA_PERM = np.array([8 * p + 4 * half + s for p in range(2) for s in range(4) for half in range(2)])


def _perm_heads(w, axis, dh=HALF):
    shape = w.shape
    w = w.reshape(shape[:axis] + (SWA_HEADS, dh) + shape[axis + 1:])
    w = jnp.take(w, jnp.asarray(_SWA_PERM), axis=axis)
    return w.reshape(shape)


def kernel(x_prompt, x_sample, cache_l0_k, cache_l0_v, cache_l1_k, cache_l1_v, cache_l2_ckv, cache_l2_kpe, cache_l3_k, cache_l3_v, c, c_ctx, w_mod, b_mod, ln_g, ln_b, w_mlp1, w_mlp2, l0_w_qkv, l0_lam, l0_subln, l0_w_o, l1_w_qkv, l1_rpb, l1_w_o, l2_w_a, l2_q_norm, l2_kv_norm, l2_w_uq, l2_w_ukv, l2_w_o, l3_w_qkv, l3_sink, l3_w_o):
    D = D_MODEL
    Bc, S, _ = x_prompt.shape
    Bl, T, _ = x_sample.shape
    P = cache_l0_k.shape[1]
    assert 1 + Bl <= MOD_ROWS and T % GRID_W == 0
    rows = T // GRID_W

    cond = jnp.concatenate([c_ctx[None], c, jnp.zeros((MOD_ROWS - 1 - Bl, D), F32)], 0)
    mod = _modulation(cond, w_mod, b_mod)
    ctx_row = lambda b: 0
    lat_row = lambda b: b + 1

    bf = lambda a: a.astype(BF16)
    tr = lambda a: jnp.swapaxes(a, -1, -2)
    w1 = bf(w_mlp1)
    w2 = bf(w_mlp2)
    qmul = HALF ** -0.5 * LOG2E

    tabs64 = _rope_tables(T, DIFF_DH, 0, 2)
    rope64 = (DIFF_DH // 4, tabs64)
    rope64_t = (DIFF_DH // 4, tuple(tr(t) for t in tabs64))

    def qkv_t(x, mod_l, row_of, w_k, w_t, nq, nk, lat, rope, name):
        if lat:
            return _project(x, mod_l, row_of, w_k, w_t, [(0, nk, rope, True, False, None)],
                            [(0, nq, rope, True, False, qmul), (nq, nk, False, True, False, None)],
                            rope=rope64 if rope else None, rope_t=rope64_t if rope else None, name=name)
        return _project(x, mod_l, row_of, w_k, w_t,
                        [(0, nk, False, True, True, None), (nk, nk, False, False, True, None)],
                        [(0, nq, False, True, False, qmul), (nq, nk, False, True, False, None)], name=name)

    def split_w(w, nq, nk):
        return bf(w[:, nq:]), bf(tr(jnp.concatenate([w[:, :nq], w[:, nq + nk:]], 1)))

    w_k, w_t = split_w(l0_w_qkv, D, D)
    wo = bf(l0_w_o)
    lam = l0_lam.astype(F32)
    subln = l0_subln.reshape(2 * DIFF_DH, 1)
    k, k0, v0, qt, vt = qkv_t(x_prompt, mod[0], ctx_row, w_k, w_t, D, D, False, False, "l0_proj_ctx")
    o = _flash(qt, k, vt, mode="diff", extra=(lam, subln), gpb=DIFF_HEADS, name="l0_attn_ctx")
    xp = _post(x_prompt, o, mod[0], ctx_row, ln_g[0], ln_b[0], wo, w1[0], w2[0], shared_mod=True, name="l0_post_ctx")
    k, qt, vt = qkv_t(x_sample, mod[0], lat_row, w_k, w_t, D, D, True, True, "l0_proj_lat")
    cache = (bf(cache_l0_k.reshape(Bl, P, D)), bf(tr(cache_l0_v.reshape(Bl, P, D))))
    o = _flash(qt, k, vt, cache, mode="diff", extra=(lam, subln), name="l0_attn_lat", **DIFF_FLASH)
    xs = _post(x_sample, o, mod[0], lat_row, ln_g[0], ln_b[0], wo, w1[0], w2[0], name="l0_post_lat")

    w_k, w_t = split_w(l1_w_qkv, D, D)
    wo = bf(l1_w_o)
    k, k1, v1, qt, vt = qkv_t(xp, mod[1], ctx_row, w_k, w_t, D, D, False, False, "l1_proj_ctx")
    o = _band(qt, [(k, vt, S, _OWN, None)], n_slab=1, tq=S, gpb=NA_HEADS // 2, name="l1_attn_ctx")
    xp = _post(xp, o, mod[1], ctx_row, ln_g[1], ln_b[1], wo, w1[1], w2[1], shared_mod=True, name="l1_post_ctx")
    k, qt, vt = qkv_t(xs, mod[1], lat_row, w_k, w_t, D, D, True, False, "l1_proj_lat")
    kc, vct = bf(cache_l1_k.reshape(Bl, P, D)), bf(tr(cache_l1_v.reshape(Bl, P, D)))
    tqn = NA_QROWS * GRID_W
    na_bias = _na_bias_blocks(l1_rpb, rows)
    o = _band(qt, [(k, vt, tqn, _PREV, 0), (k, vt, tqn, _OWN, tqn), (k, vt, tqn, _NEXT, 2 * tqn),
                   (kc, vct, P, _FIRST, None)], n_slab=1, tq=tqn, gpb=NA_HEADS // 2, na_bias=na_bias, name="l1_attn_lat")
    xs = _post(xs, o, mod[1], lat_row, ln_g[1], ln_b[1], wo, w1[1], w2[1], name="l1_post_lat")

    H, dq = MLA_HEADS, MLA_NOPE + MLA_ROPE
    wa = bf(jnp.pad(l2_w_a, ((0, 0), (0, -l2_w_a.shape[1] % LANES))))
    wuq = jnp.pad(l2_w_uq.reshape(MLA_Q_RANK, H, dq), ((0, 0), (0, 0), (0, LANES - dq))).reshape(MLA_Q_RANK, H * LANES)
    wuqt = bf(tr(wuq))
    wukv = l2_w_ukv.reshape(MLA_KV_RANK, H, MLA_NOPE + MLA_VDIM)
    wk_n = jnp.pad(wukv[:, :, :MLA_NOPE], ((0, 0), (0, 0), (0, LANES - MLA_NOPE))).reshape(MLA_KV_RANK, H * LANES)
    place = np.zeros((LANES, H, LANES), np.float32)
    place[np.arange(MLA_ROPE), :, MLA_NOPE + np.arange(MLA_ROPE)] = 1.0
    wk = bf(jnp.concatenate([wk_n, jnp.asarray(place.reshape(LANES, H * LANES))], 0))
    wvt = bf(tr(wukv[:, :, MLA_NOPE:].reshape(MLA_KV_RANK, H * MLA_VDIM)))
    wo = bf(l2_w_o)
    qn, kvn = l2_q_norm.reshape(1, -1), l2_kv_norm.reshape(1, -1)
    mscale = dq ** -0.5 * LOG2E
    qt, k, vt, ckv2, kpe2 = _mla_project(xp, mod[2], ctx_row, wa, qn, kvn, wuqt, wk, wvt, mscale, name="l2_proj_ctx")
    o = _flash(qt, k, vt, mode="mla", gpb=MLA_HEADS // 2, name="l2_attn_ctx")
    xp = _post(xp, o, mod[2], ctx_row, ln_g[2], ln_b[2], wo, w1[2], w2[2], shared_mod=True, name="l2_post_ctx")
    tabs = tuple(tr(t) for t in _rope_tables(T, MLA_ROPE, MLA_NOPE, 1)) + _rope_tables(T, MLA_ROPE, 0, 1)
    qt, k, vt = _mla_project(xs, mod[2], lat_row, wa, qn, kvn, wuqt, wk, wvt, mscale, rope=tabs, name="l2_proj_lat")
    kpe_slab = jnp.pad(cache_l2_kpe, ((0, 0), (0, 0), (0, LANES - MLA_ROPE)))
    cache = _mla_cache_expand(cache_l2_ckv, kpe_slab, wk, wvt)
    o = _flash(qt, k, vt, cache, mode="mla", name="l2_attn_lat", **MLA_FLASH)
    xs = _post(xs, o, mod[2], lat_row, ln_g[2], ln_b[2], wo, w1[2], w2[2], name="l2_post_lat")

    nq3, nk3 = SWA_HEADS * HALF, SWA_KV_HEADS * HALF
    w = jnp.concatenate([_perm_heads(l3_w_qkv[:, :nq3], 1), l3_w_qkv[:, nq3:]], 1)
    w_k, w_t = split_w(w, nq3, nk3)
    wo = bf(_perm_heads(l3_w_o, 0))
    sink = jnp.take(l3_sink.astype(F32), jnp.asarray(_SWA_PERM)) * LOG2E

    def sink_rows(tq):
        return jnp.repeat(sink.reshape(2, 1, 8), tq, axis=-1)

    k, k3, v3, qt, vt = qkv_t(xp, mod[3], ctx_row, w_k, w_t, nq3, nk3, False, False, "l3_proj_ctx")
    o = _band(qt, [(k, vt, S, _OWN, None)], n_slab=4, tq=S, gpb=2, sink=sink_rows(S), name="l3_attn_ctx")
    xp = _post(xp, o, mod[3], ctx_row, ln_g[3], ln_b[3], wo, w1[3], w2[3], shared_mod=True, name="l3_post_ctx")
    k, qt, vt = qkv_t(xs, mod[3], lat_row, w_k, w_t, nq3, nk3, True, True, "l3_proj_lat")
    kc, vct = bf(cache_l3_k.reshape(Bl, P, nk3)), bf(tr(cache_l3_v.reshape(Bl, P, nk3)))
    mask = _swa_mask_tables(T // SWA_QB, 8)
    o = _band(qt, [(k, vt, SWA_QB, _PREV, 0), (k, vt, SWA_QB, _OWN, SWA_QB), (k, vt, SWA_QB, _NEXT, 2 * SWA_QB),
                   (kc, vct, P, _FIRST, None)], n_slab=4, tq=SWA_QB, gpb=2, bias=mask, sink=sink_rows(SWA_QB),
              name="l3_attn_lat")
    xs = _post(xs, o, mod[3], lat_row, ln_g[3], ln_b[3], wo, w1[3], w2[3], name="l3_post_lat")

    return (xp, xs,
            k0.reshape(Bc, S, DIFF_HEADS, 2 * DIFF_DH), v0.reshape(Bc, S, DIFF_HEADS, 2 * DIFF_DH),
            k1.reshape(Bc, S, NA_HEADS, HALF), v1.reshape(Bc, S, NA_HEADS, HALF),
            ckv2, kpe2,
            k3.reshape(Bc, S, SWA_KV_HEADS, HALF), v3.reshape(Bc, S, SWA_KV_HEADS, HALF))
```

```python
import functools
import math

import numpy as np
import jax
import jax.numpy as jnp
from jax import lax
from jax.experimental import pallas as pl
from jax.experimental.pallas import tpu as pltpu

D_MODEL = 1024
DEPTH = 4
GRID_W = 64
ROPE_BASE = 10000.0
LN_EPS = 1e-5
NEG_INF = -1e30
ALPHA = (2 * DEPTH) ** 0.25
D_FF = 4 * D_MODEL
DIFF_DH = 64
DIFF_HEADS = 8
DIFF_LAMBDA_INIT = 0.8 - 0.6 * math.exp(-0.3 * 0)
NA_HEADS = 16
NA_KH = 8
NA_KW = 16
NA_QROWS = 4
MLA_HEADS = 16
MLA_NOPE = 64
MLA_ROPE = 32
MLA_VDIM = 64
MLA_Q_RANK = 512
MLA_KV_RANK = 256
SWA_HEADS = 16
SWA_KV_HEADS = 4
SWA_WINDOW = 128
SWA_QB = 128
LOG2E = math.log2(math.e)

LANES = 128
HALF = LANES // 2
BF16_ROWS = 16
MOD_ROWS = 16
VMEM_LIMIT = 56 * 1024 * 1024

MOD_COLS = 2048
PROJ_ROWS, PROJ_COLS = 512, 256
MLA_PROJ_COLS = 512
POST_ROWS, POST_SUB_ROWS, POST_FF_COLS = 512, 256, 1024
FLASH_QUERIES, FLASH_KEYS = 1024, 512
DIFF_FLASH = dict(gpb=2, tq=512, tk=512)
MLA_FLASH = dict(gpb=2, tq=1024, tk=256)

F32 = jnp.float32
BF16 = jnp.bfloat16


def _cparams(sem):
    return pltpu.CompilerParams(dimension_semantics=sem, vmem_limit_bytes=VMEM_LIMIT)


def _dot(a, b):
    return jnp.dot(a, b, preferred_element_type=F32)


def _layer_norm(x, g, b):
    mu = jnp.mean(x, -1, keepdims=True)
    xc = x - mu
    var = jnp.mean(xc * xc, -1, keepdims=True)
    return xc * lax.rsqrt(var + LN_EPS) * g + b


def _rms_norm(x, g, axis=-1):
    return x * lax.rsqrt(jnp.mean(x * x, axis, keepdims=True) + LN_EPS) * g


def _rope_slab(x, cos, sin_up, sin_dn, n):
    return x * cos + pltpu.roll(x, LANES - n, 1) * sin_up + pltpu.roll(x, n, 1) * sin_dn


def _rope_slab_t(x, cos, sin_up, sin_dn, n):
    up = jnp.concatenate([x[n:], x[:n]], axis=0)
    dn = jnp.concatenate([x[LANES - n:], x[:LANES - n]], axis=0)
    return x * cos + up * sin_up + dn * sin_dn


def _split_heads_t(qt):
    lo = lax.broadcasted_iota(jnp.int32, qt.shape, 0) < HALF
    zero = jnp.zeros_like(qt)
    return jnp.concatenate([jnp.where(lo, qt, zero), jnp.where(lo, zero, qt)], axis=1)


def _with_ones_row(vt):
    r = lax.broadcasted_iota(jnp.int32, (BF16_ROWS, vt.shape[1]), 0)
    return jnp.concatenate([vt, jnp.where(r == 0, 1.0, 0.0).astype(BF16)], axis=0)


def _modulated(x_ref, mod_ref):
    return x_ref[0] * (1.0 + mod_ref[0, 1:2, :]) + mod_ref[0, 0:1, :]


def _mod_kernel(c_ref, w_ref, b_ref, o_ref):
    c = c_ref[...]
    s = c / (1.0 + jnp.exp(-c))
    o_ref[0] = _dot(s.astype(BF16), w_ref[0].astype(BF16)) + b_ref[0]


def _modulation(cond, w_mod, b_mod):
    tn = MOD_COLS
    n = 6 * D_MODEL
    out = pl.pallas_call(
        _mod_kernel,
        grid=(DEPTH, n // tn),
        in_specs=[pl.BlockSpec((MOD_ROWS, D_MODEL), lambda l, j: (0, 0)),
                  pl.BlockSpec((1, D_MODEL, tn), lambda l, j: (l, 0, j)),
                  pl.BlockSpec((1, 1, tn), lambda l, j: (l, 0, j))],
        out_specs=pl.BlockSpec((1, MOD_ROWS, tn), lambda l, j: (l, 0, j)),
        out_shape=jax.ShapeDtypeStruct((DEPTH, MOD_ROWS, n), F32),
        compiler_params=_cparams(("parallel", "parallel")),
        name="modulation",
    )(cond, w_mod, b_mod.reshape(DEPTH, 1, n))
    return out.reshape(DEPTH, MOD_ROWS, 6, D_MODEL)


def _proj_kernel(*refs, segs, tsegs, rope_n, cw):
    x_ref, mod_ref = refs[:2]
    pos = 2
    w_ref = wt_ref = None
    if segs:
        w_ref = refs[pos]
        pos += 1
    if tsegs:
        wt_ref = refs[pos]
        pos += 1
    if rope_n and any(s[2] for s in segs):
        cos, sup, sdn = (r[...] for r in refs[pos:pos + 3])
        pos += 3
    if rope_n and any(s[2] for s in tsegs):
        cos_t, sup_t, sdn_t = (r[...] for r in refs[pos:pos + 3])
        pos += 3
    outs = list(refs[pos:])
    h32 = _modulated(x_ref, mod_ref)
    h = h32.astype(BF16)
    for (start, width, rope, want16, want32, mul) in segs:
        o16 = outs.pop(0) if want16 else None
        o32 = outs.pop(0) if want32 else None
        for c in range(0, width, cw):
            acc = _dot(h, w_ref[:, start + c:start + c + cw])
            if want32:
                o32[0, :, c:c + cw] = acc
            if mul is not None:
                acc = acc * mul
            if rope:
                acc = jnp.concatenate(
                    [_rope_slab(acc[:, s:s + LANES], cos, sup, sdn, rope_n) for s in range(0, cw, LANES)], axis=1)
            if want16:
                o16[0, :, c:c + cw] = acc.astype(BF16)
    if tsegs:
        ht = h32.T.astype(BF16)
        for (start, width, rope, _, _, mul) in tsegs:
            o16 = outs.pop(0)
            for c in range(0, width, cw):
                acc = _dot(wt_ref[start + c:start + c + cw, :], ht)
                if mul is not None:
                    acc = acc * mul
                if rope:
                    acc = jnp.concatenate(
                        [_rope_slab_t(acc[s:s + LANES], cos_t, sup_t, sdn_t, rope_n) for s in range(0, cw, LANES)],
                        axis=0)
                o16[0, c:c + cw, :] = acc.astype(BF16)


def _project(x, mod, row_of, w, wt, segs, tsegs, rope=None, rope_t=None, tm=PROJ_ROWS, cw=PROJ_COLS, name="proj"):
    B, T, _ = x.shape
    tm = min(tm, T)
    cw = min([cw] + [s[1] for s in list(segs) + list(tsegs)])
    const = lambda b, i: (0, 0)
    in_specs = [pl.BlockSpec((1, tm, D_MODEL), lambda b, i: (b, i, 0)),
                pl.BlockSpec((1, 6, D_MODEL), lambda b, i: (row_of(b), 0, 0))]
    args = [x, mod]
    if segs:
        in_specs.append(pl.BlockSpec(w.shape, const))
        args.append(w)
    if tsegs:
        in_specs.append(pl.BlockSpec(wt.shape, const))
        args.append(wt)
    rope_n = 0
    if rope is not None:
        rope_n, tabs = rope
        in_specs += [pl.BlockSpec((tm, LANES), lambda b, i: (i, 0))] * 3
        args += list(tabs)
    if rope_t is not None:
        rope_n, tabs = rope_t
        in_specs += [pl.BlockSpec((LANES, tm), lambda b, i: (0, i))] * 3
        args += list(tabs)
    out_specs, out_shape = [], []
    for (_, width, _, want16, want32, _) in segs:
        for want, dt in ((want16, BF16), (want32, F32)):
            if want:
                out_specs.append(pl.BlockSpec((1, tm, width), lambda b, i: (b, i, 0)))
                out_shape.append(jax.ShapeDtypeStruct((B, T, width), dt))
    for (_, width, _, _, _, _) in tsegs:
        out_specs.append(pl.BlockSpec((1, width, tm), lambda b, i: (b, 0, i)))
        out_shape.append(jax.ShapeDtypeStruct((B, width, T), BF16))
    return pl.pallas_call(
        functools.partial(_proj_kernel, segs=tuple(segs), tsegs=tuple(tsegs), rope_n=rope_n, cw=cw),
        grid=(B, T // tm), in_specs=in_specs, out_specs=out_specs, out_shape=out_shape,
        compiler_params=_cparams(("parallel", "parallel")), name=name,
    )(*args)


def _mla_proj_kernel(*refs, lat, scale, cw):
    x_ref, mod_ref, wa_ref, qn_ref, kvn_ref, wuqt_ref, wk_ref, wvt_ref = refs[:8]
    pos = 8
    if lat:
        qcos, qsup, qsdn, kcos, ksup, ksdn = (r[...] for r in refs[8:14])
        pos = 14
    qt_out, k_out, vt_out = refs[pos:pos + 3]
    h = _modulated(x_ref, mod_ref).astype(BF16)
    a = _dot(h, wa_ref[...])
    cq = _rms_norm(a[:, :MLA_Q_RANK], qn_ref[...])
    ckv = _rms_norm(a[:, MLA_Q_RANK:MLA_Q_RANK + MLA_KV_RANK], kvn_ref[...])
    kpe = a[:, MLA_Q_RANK + MLA_KV_RANK:]
    if lat:
        kpe = _rope_slab(kpe, kcos, ksup, ksdn, MLA_ROPE // 4)
    else:
        ckv_state, kpe_state = refs[pos + 3:pos + 5]
        ckv_state[0] = ckv
        kpe_state[0] = kpe[:, :MLA_ROPE]
    cqt = cq.T.astype(BF16)
    n_q = wuqt_ref.shape[0]
    for c in range(0, n_q, cw):
        q = _dot(wuqt_ref[c:c + cw, :], cqt) * scale
        if lat:
            q = jnp.concatenate(
                [_rope_slab_t(q[s:s + LANES], qcos, qsup, qsdn, MLA_ROPE // 4) for s in range(0, cw, LANES)], axis=0)
        qt_out[0, c:c + cw, :] = q.astype(BF16)
    kin = jnp.concatenate([ckv.astype(BF16), kpe.astype(BF16)], axis=1)
    for c in range(0, wk_ref.shape[1], cw):
        k_out[0, :, c:c + cw] = _dot(kin, wk_ref[:, c:c + cw]).astype(BF16)
    ckvt = ckv.T.astype(BF16)
    for c in range(0, wvt_ref.shape[0], cw):
        vt_out[0, c:c + cw, :] = _dot(wvt_ref[c:c + cw, :], ckvt).astype(BF16)


def _mla_project(x, mod, row_of, wa, qn, kvn, wuqt, wk, wvt, scale, rope=None, tm=PROJ_ROWS, name="mla_proj"):
    B, T, _ = x.shape
    tm = min(tm, T)
    lat = rope is not None
    const = lambda b, i: (0, 0)
    in_specs = [pl.BlockSpec((1, tm, D_MODEL), lambda b, i: (b, i, 0)),
                pl.BlockSpec((1, 6, D_MODEL), lambda b, i: (row_of(b), 0, 0))]
    in_specs += [pl.BlockSpec(a.shape, const) for a in (wa, qn, kvn, wuqt, wk, wvt)]
    args = [x, mod, wa, qn, kvn, wuqt, wk, wvt]
    if lat:
        in_specs += [pl.BlockSpec((LANES, tm), lambda b, i: (0, i))] * 3
        in_specs += [pl.BlockSpec((tm, LANES), lambda b, i: (i, 0))] * 3
        args += list(rope)
    tok = lambda b, i: (b, i, 0)
    feat = lambda b, i: (b, 0, i)
    out_specs = [pl.BlockSpec((1, wuqt.shape[0], tm), feat), pl.BlockSpec((1, tm, wk.shape[1]), tok),
                 pl.BlockSpec((1, wvt.shape[0], tm), feat)]
    out_shape = [jax.ShapeDtypeStruct((B, wuqt.shape[0], T), BF16), jax.ShapeDtypeStruct((B, T, wk.shape[1]), BF16),
                 jax.ShapeDtypeStruct((B, wvt.shape[0], T), BF16)]
    if not lat:
        out_specs += [pl.BlockSpec((1, tm, MLA_KV_RANK), tok), pl.BlockSpec((1, tm, MLA_ROPE), tok)]
        out_shape += [jax.ShapeDtypeStruct((B, T, MLA_KV_RANK), F32), jax.ShapeDtypeStruct((B, T, MLA_ROPE), F32)]
    return pl.pallas_call(
        functools.partial(_mla_proj_kernel, lat=lat, scale=scale, cw=MLA_PROJ_COLS),
        grid=(B, T // tm), in_specs=in_specs, out_specs=out_specs, out_shape=out_shape,
        compiler_params=_cparams(("parallel", "parallel")), name=name,
    )(*args)


def _mla_cache_kernel(ckv_ref, kpe_ref, wk_ref, wvt_ref, k_out, vt_out):
    ckv = ckv_ref[0]
    kin = jnp.concatenate([ckv.astype(BF16), kpe_ref[0].astype(BF16)], axis=1)
    k_out[0] = _dot(kin, wk_ref[...]).astype(BF16)
    vt_out[0] = _dot(wvt_ref[...], ckv.T.astype(BF16)).astype(BF16)


def _mla_cache_expand(c_ckv, c_kpe_slab, wk, wvt):
    B, P, _ = c_ckv.shape
    const = lambda b: (0, 0)
    return pl.pallas_call(
        _mla_cache_kernel, grid=(B,),
        in_specs=[pl.BlockSpec((1, P, MLA_KV_RANK), lambda b: (b, 0, 0)),
                  pl.BlockSpec((1, P, LANES), lambda b: (b, 0, 0)),
                  pl.BlockSpec(wk.shape, const), pl.BlockSpec(wvt.shape, const)],
        out_specs=[pl.BlockSpec((1, P, wk.shape[1]), lambda b: (b, 0, 0)),
                   pl.BlockSpec((1, wvt.shape[0], P), lambda b: (b, 0, 0))],
        out_shape=[jax.ShapeDtypeStruct((B, P, wk.shape[1]), BF16),
                   jax.ShapeDtypeStruct((B, wvt.shape[0], P), BF16)],
        compiler_params=_cparams(("parallel",)), name="mla_cache_expand",
    )(c_ckv, c_kpe_slab, wk, wvt)


def _flash_kernel(*refs, mode, has_cache, gpb, tq, tk):
    qt_ref = refs[0]
    pos = 1
    if has_cache:
        kc_ref, vct_ref = refs[1:3]
        pos = 3
    k_ref, vt_ref = refs[pos:pos + 2]
    pos += 2
    if mode == "diff":
        lam_ref, subln_ref = refs[pos:pos + 2]
        pos += 2
    o_ref, s_sc, acc_sc = refs[pos:pos + 3]
    dv = LANES if mode == "diff" else HALF
    wq = LANES if mode == "diff" else 2 * LANES

    blocks = []
    if has_cache:
        pc = kc_ref.shape[1]
        blocks += [(kc_ref, vct_ref, j, min(tk, pc - j)) for j in range(0, pc, tk)]
    blocks += [(k_ref, vt_ref, j, tk) for j in range(0, k_ref.shape[1], tk)]

    qts = [qt_ref[0, g * wq:(g + 1) * wq, :] for g in range(gpb)]
    if mode == "diff":
        qts = [_split_heads_t(qt) for qt in qts]

    def logits(n, g):
        kr, _, j, nk = blocks[n]
        k = kr[0, j:j + nk, g * wq:(g + 1) * wq]
        buf = 2 * g + n % 2
        if mode == "diff":
            s_sc[buf, :nk, :] = _dot(k, qts[g])
        else:
            s_sc[buf, :nk, :tq] = _dot(k[:, :LANES], qts[g][:LANES])
            s_sc[buf, :nk, tq:] = _dot(k[:, LANES:], qts[g][LANES:])

    for g in range(gpb):
        logits(0, g)
    m = [None] * gpb
    for n in range(len(blocks)):
        for g in range(gpb):
            if n + 1 < len(blocks):
                logits(n + 1, g)
            _, vr, j, nk = blocks[n]
            vt = vr[0, g * LANES:(g + 1) * LANES, j:j + nk]
            s = s_sc[2 * g + n % 2, :nk, :]
            m_cur = jnp.max(s, 0, keepdims=True)
            m_new = m_cur if m[g] is None else jnp.maximum(m[g], m_cur)
            pb = jnp.exp2(s - m_new).astype(BF16)
            if mode == "diff":
                pv = _dot(_with_ones_row(vt), pb)
            else:
                pv = jnp.concatenate([_dot(_with_ones_row(vt[:HALF]), pb[:, :tq]),
                                      _dot(_with_ones_row(vt[HALF:]), pb[:, tq:])], axis=1)
            if m[g] is None:
                acc_sc[g] = pv
            else:
                acc_sc[g] = jnp.exp2(m[g] - m_new) * acc_sc[g] + pv
            m[g] = m_new

    if mode == "diff":
        lam = lam_ref[...]
        lam_full = (jnp.exp(jnp.sum(lam[0:1] * lam[1:2], keepdims=True))
                    - jnp.exp(jnp.sum(lam[2:3] * lam[3:4], keepdims=True)) + DIFF_LAMBDA_INIT)
    for g in range(gpb):
        acc = acc_sc[g]
        o = acc[:dv] / acc[dv:dv + 1]
        o0, o1 = o[:, :tq], o[:, tq:]
        if mode == "diff":
            out_t = _rms_norm(o0 - lam_full * o1, subln_ref[...], axis=0) * (1.0 - DIFF_LAMBDA_INIT)
        else:
            out_t = jnp.concatenate([o0, o1], axis=0)
        o_ref[0, :, g * LANES:(g + 1) * LANES] = out_t.T.astype(o_ref.dtype)


def _flash(qt, k, vt, cache=None, *, mode, extra=(), gpb=1, tq=FLASH_QUERIES, tk=FLASH_KEYS, name="flash"):
    B, _, T = qt.shape
    S = k.shape[1]
    wq = LANES if mode == "diff" else 2 * LANES
    G = vt.shape[1] // LANES
    tq, tk = min(tq, T), min(tk, S)
    in_specs = [pl.BlockSpec((1, gpb * wq, tq), lambda b, g, i: (b, g, i))]
    args = [qt]
    if cache is not None:
        kc, vct = cache
        P = kc.shape[1]
        in_specs += [pl.BlockSpec((1, P, gpb * wq), lambda b, g, i: (b, 0, g)),
                     pl.BlockSpec((1, gpb * LANES, P), lambda b, g, i: (b, g, 0))]
        args += [kc, vct]
    in_specs += [pl.BlockSpec((1, S, gpb * wq), lambda b, g, i: (b, 0, g)),
                 pl.BlockSpec((1, gpb * LANES, S), lambda b, g, i: (b, g, 0))]
    args += [k, vt]
    for e in extra:
        in_specs.append(pl.BlockSpec(e.shape, lambda b, g, i: (0, 0)))
        args.append(e)
    dv = LANES if mode == "diff" else HALF
    return pl.pallas_call(
        functools.partial(_flash_kernel, mode=mode, has_cache=cache is not None, gpb=gpb, tq=tq, tk=tk),
        grid=(B, G // gpb, T // tq), in_specs=in_specs,
        out_specs=pl.BlockSpec((1, tq, gpb * LANES), lambda b, g, i: (b, i, g)),
        out_shape=jax.ShapeDtypeStruct((B, T, G * LANES), BF16),
        scratch_shapes=[pltpu.VMEM((2 * gpb, tk, 2 * tq), F32), pltpu.VMEM((gpb, dv + BF16_ROWS, 2 * tq), F32)],
        compiler_params=_cparams(("parallel", "parallel", "parallel")), name=name,
    )(*args)


def _band_kernel(*refs, n_seg, seg_bias, n_slab, gpb, bias_kind, has_sink, tq):
    qt_ref = refs[0]
    kv = refs[1:1 + 2 * n_seg]
    pos = 1 + 2 * n_seg
    sink_ref = None
    if bias_kind == "table":
        bias_ref = refs[pos]
        pos += 1
    elif bias_kind == "na":
        blk_ref, rowok_ref = refs[pos:pos + 2]
        pos += 2
    if has_sink:
        sink_ref = refs[pos]
        pos += 1
    o_ref, s_sc = refs[pos:pos + 2]
    rows = [kv[2 * i].shape[1] for i in range(n_seg)]
    offs = [sum(rows[:i]) for i in range(n_seg)]

    def bias(g, off, n):
        if bias_kind == "table":
            return bias_ref[0, 0, off:off + n, :]
        tiles = []
        for kb in range(off // GRID_W, (off + n) // GRID_W):
            tiles.append(jnp.concatenate(
                [blk_ref[g, x, kb - 2 * p + 2] + rowok_ref[0, 2 * kb + p:2 * kb + p + 1, :]
                 for x in range(2) for p in range(NA_QROWS // 2)], axis=1))
        return jnp.concatenate(tiles, axis=0)

    def logits(g):
        q0 = g * n_slab * LANES
        qst = jnp.concatenate(
            [_split_heads_t(qt_ref[0, q0 + s * LANES:q0 + (s + 1) * LANES, :]) for s in range(n_slab)], axis=1)
        k = jnp.concatenate([kv[2 * i][0, :, g * LANES:(g + 1) * LANES] for i in range(n_seg)], axis=0)
        lg = _dot(k, qst)
        for i in range(n_seg):
            part = lg[offs[i]:offs[i] + rows[i]]
            if seg_bias[i] is not None:
                part = part + bias(g, seg_bias[i], rows[i])
            s_sc[g % 2, offs[i]:offs[i] + rows[i], :] = part

    logits(0)
    for g in range(gpb):
        if g + 1 < gpb:
            logits(g + 1)
        q0 = g * n_slab * LANES
        s = s_sc[g % 2]
        m = jnp.max(s, 0, keepdims=True)
        if has_sink:
            sink = sink_ref[g]
            m = jnp.maximum(m, sink)
        vt = jnp.concatenate([kv[2 * i + 1][0, g * LANES:(g + 1) * LANES, :] for i in range(n_seg)], axis=1)
        acc = _dot(_with_ones_row(vt), jnp.exp2(s - m).astype(BF16))
        l = acc[LANES:LANES + 1]
        if has_sink:
            l = l + jnp.exp2(sink - m)
        o = acc[:LANES] / l
        for s in range(n_slab):
            c = 2 * s * tq
            out_t = jnp.concatenate([o[:HALF, c:c + tq], o[HALF:, c + tq:c + 2 * tq]], axis=0)
            o_ref[0, :, q0 + s * LANES:q0 + (s + 1) * LANES] = out_t.T.astype(o_ref.dtype)


def _band(qt, segs, *, n_slab, tq, gpb=1, bias=None, na_bias=None, sink=None, name="band"):
    B, _, T = qt.shape
    G = qt.shape[1] // (n_slab * LANES)
    nq = T // tq
    R = n_slab * 2 * tq
    in_specs = [pl.BlockSpec((1, gpb * n_slab * LANES, tq), lambda g, b, i: (b, g, i))]
    args = [qt]
    seg_bias = []
    for (k, vt, ts, blk_of, boff) in segs:
        in_specs += [pl.BlockSpec((1, ts, gpb * LANES), (lambda f: lambda g, b, i: (b, f(i, nq), g))(blk_of)),
                     pl.BlockSpec((1, gpb * LANES, ts), (lambda f: lambda g, b, i: (b, g, f(i, nq)))(blk_of))]
        args += [k, vt]
        seg_bias.append(boff)
    var_of = lambda i: jnp.where(i == 0, 0, jnp.where(i == nq - 1, 2, 1))
    bias_kind = None
    if bias is not None:
        bias_kind = "table"
        in_specs.append(pl.BlockSpec((1, 1, bias.shape[2], R), lambda g, b, i: (var_of(i), 0, 0, 0)))
        args.append(bias)
    elif na_bias is not None:
        bias_kind = "na"
        blk, rowok = na_bias
        in_specs += [pl.BlockSpec((gpb,) + blk.shape[1:], lambda g, b, i: (g, 0, 0, 0, 0)),
                     pl.BlockSpec((1,) + rowok.shape[1:], lambda g, b, i: (var_of(i), 0, 0))]
        args += [blk, rowok]
    if sink is not None:
        in_specs.append(pl.BlockSpec((gpb, 1, R), lambda g, b, i: (g, 0, 0)))
        args.append(sink)
    return pl.pallas_call(
        functools.partial(_band_kernel, n_seg=len(segs), seg_bias=tuple(seg_bias), n_slab=n_slab, gpb=gpb,
                          bias_kind=bias_kind, has_sink=sink is not None, tq=tq),
        grid=(G // gpb, B, nq), in_specs=in_specs,
        out_specs=pl.BlockSpec((1, tq, gpb * n_slab * LANES), lambda g, b, i: (b, i, g)),
        out_shape=jax.ShapeDtypeStruct((B, T, G * n_slab * LANES), BF16),
        scratch_shapes=[pltpu.VMEM((2, sum(s[2] for s in segs), R), F32)],
        compiler_params=_cparams(("parallel", "parallel", "parallel")), name=name,
    )(*args)


_OWN = lambda i, nq: i
_PREV = lambda i, nq: jnp.maximum(i - 1, 0)
_NEXT = lambda i, nq: jnp.minimum(i + 1, nq - 1)
_FIRST = lambda i, nq: 0


def _post_kernel(x_ref, o_ref, mod_ref, g_ref, b_ref, wo_ref, w1_ref, w2_ref, out_ref, *, cf, rows):
    subs = [slice(r, r + rows) for r in range(0, x_ref.shape[1], rows)]
    ys = [_dot(o_ref[0, s, :], wo_ref[...]) for s in subs]
    x1s = [_layer_norm(ALPHA * x_ref[0, s, :] + mod_ref[0, 2:3, :] * y, g_ref[0:1, :], b_ref[0:1, :])
           for s, y in zip(subs, ys)]
    hs = [(x1 * (1.0 + mod_ref[0, 4:5, :]) + mod_ref[0, 3:4, :]).astype(BF16) for x1 in x1s]
    accs = []
    for h in hs:
        acc = jnp.zeros(h.shape, F32)
        for c in range(0, D_FF, cf):
            u = jnp.maximum(_dot(h, w1_ref[:, c:c + cf]), 0.0)
            acc = acc + _dot((u * u).astype(BF16), w2_ref[c:c + cf, :])
        accs.append(acc)
    for s, x1, acc in zip(subs, x1s, accs):
        out_ref[0, s, :] = _layer_norm(ALPHA * x1 + mod_ref[0, 5:6, :] * acc, g_ref[1:2, :], b_ref[1:2, :])


def _post(x, o, mod, row_of, ln_g, ln_b, wo, w1, w2, tm=POST_ROWS, shared_mod=False, name="post"):
    shape = x.shape
    if shared_mod and shape[1] < tm and (shape[0] * shape[1]) % tm == 0:
        x, o = x.reshape(-1, tm, D_MODEL), o.reshape(-1, tm, D_MODEL)
    B, T, _ = x.shape
    tm = min(tm, T)
    const = lambda b, i: (0, 0)
    once = pl.Buffered(1)
    out = pl.pallas_call(
        functools.partial(_post_kernel, cf=POST_FF_COLS, rows=min(tm, POST_SUB_ROWS)),
        grid=(B, T // tm),
        in_specs=[pl.BlockSpec((1, tm, D_MODEL), lambda b, i: (b, i, 0)),
                  pl.BlockSpec((1, tm, D_MODEL), lambda b, i: (b, i, 0)),
                  pl.BlockSpec((1, 6, D_MODEL), lambda b, i: (row_of(b), 0, 0)),
                  pl.BlockSpec((2, D_MODEL), const), pl.BlockSpec((2, D_MODEL), const),
                  pl.BlockSpec(wo.shape, const, pipeline_mode=once),
                  pl.BlockSpec(w1.shape, const, pipeline_mode=once),
                  pl.BlockSpec(w2.shape, const, pipeline_mode=once)],
        out_specs=pl.BlockSpec((1, tm, D_MODEL), lambda b, i: (b, i, 0)),
        out_shape=jax.ShapeDtypeStruct(x.shape, F32),
        compiler_params=_cparams(("parallel", "parallel")), name=name,
    )(x, o, mod, ln_g, ln_b, wo, w1, w2)
    return out.reshape(shape)


def _rope_tables(T, R, lead, reps):
    n = R // 4
    t = jnp.arange(T)
    inv = ROPE_BASE ** (-jnp.arange(n, dtype=jnp.float32) / n)

    def cs(p):
        ang = p.astype(jnp.float32)[:, None] * inv
        return jnp.cos(ang), jnp.sin(ang)

    cr, sr = cs(t // GRID_W)
    cc, sc = cs(t % GRID_W)
    z = jnp.zeros_like(sr)
    cos = jnp.concatenate([cr, cr, cc, cc], -1)
    sup = jnp.concatenate([-sr, z, -sc, z], -1)
    sdn = jnp.concatenate([z, sr, z, sc], -1)
    tail = LANES - lead - reps * R

    def slab(a, fill):
        parts = [jnp.full((T, lead), fill, F32)] if lead else []
        parts += [a] * reps
        if tail:
            parts.append(jnp.full((T, tail), fill, F32))
        return jnp.concatenate(parts, -1)

    return slab(cos, 1.0), slab(sup, 0.0), slab(sdn, 0.0)


def _na_bias_blocks(rpb, rows):
    qr, kb_n = NA_QROWS, 3 * NA_QROWS
    n_dr, n_dc = 2 * NA_KH - 1, 2 * NA_KW - 1
    c = np.arange(GRID_W)
    c0 = np.clip(c - NA_KW // 2, 0, GRID_W - NA_KW)
    kc = np.arange(GRID_W)[:, None]
    col_ok = (kc >= c0[None]) & (kc < c0[None] + NA_KW)
    sel_c = ((np.arange(n_dc)[:, None, None] == (kc - c[None] + NA_KW - 1)[None]) & col_ok[None])
    H = rpb.shape[0]
    rpb2 = (rpb.astype(F32) * LOG2E).reshape(H // 2, 2, n_dr, n_dc)
    u = jnp.einsum("gxde,ejc->gxdjc", rpb2, jnp.asarray(sel_c, F32), precision=lax.Precision.HIGHEST)
    u = u + jnp.asarray(np.where(col_ok, 0.0, NEG_INF).astype(np.float32))
    blocks = jnp.concatenate([u[:, :, 1:], u[:, :, :-1]], axis=-1)
    rowok = np.zeros((3, kb_n, qr), np.float32)
    for v, r_base in enumerate((0, qr, rows - qr)):
        r = r_base + np.arange(qr)[None]
        kr = r_base - qr + np.arange(kb_n)[:, None]
        r0 = np.clip(r - NA_KH // 2, 0, rows - NA_KH)
        rowok[v] = np.where((kr >= r0) & (kr < r0 + NA_KH), 0.0, NEG_INF)
    rowok = np.repeat(rowok.reshape(3, kb_n * qr // 2, 2), GRID_W, axis=-1)
    return blocks, jnp.asarray(rowok)


def _swa_mask_tables(nb, reps):
    qi = np.arange(SWA_QB)[None, :]
    kj = np.arange(3 * SWA_QB)[:, None]
    tabs = []
    for blk in (0, 1, nb - 1):
        kpos = (blk - 1) * SWA_QB + kj
        ok = (np.abs(kj - SWA_QB - qi) <= SWA_WINDOW) & (kpos >= 0) & (kpos < nb * SWA_QB)
        tabs.append(np.tile(np.where(ok, 0.0, NEG_INF).astype(np.float32), (1, reps))[None])
    return jnp.asarray(np.stack(tabs, 0))


_SWA_PERM = np.array([8 * p + 4 * half + s for p in range(2) for s in range(4) for half in range(2)])


def _perm_heads(w, axis, dh=HALF):
    shape = w.shape
    w = w.reshape(shape[:axis] + (SWA_HEADS, dh) + shape[axis + 1:])
    w = jnp.take(w, jnp.asarray(_SWA_PERM), axis=axis)
    return w.reshape(shape)


def kernel(x_prompt, x_sample, cache_l0_k, cache_l0_v, cache_l1_k, cache_l1_v, cache_l2_ckv, cache_l2_kpe, cache_l3_k, cache_l3_v, c, c_ctx, w_mod, b_mod, ln_g, ln_b, w_mlp1, w_mlp2, l0_w_qkv, l0_lam, l0_subln, l0_w_o, l1_w_qkv, l1_rpb, l1_w_o, l2_w_a, l2_q_norm, l2_kv_norm, l2_w_uq, l2_w_ukv, l2_w_o, l3_w_qkv, l3_sink, l3_w_o):
    D = D_MODEL
    Bc, S, _ = x_prompt.shape
    Bl, T, _ = x_sample.shape
    P = cache_l0_k.shape[1]
    assert 1 + Bl <= MOD_ROWS and T % GRID_W == 0
    rows = T // GRID_W

    cond = jnp.concatenate([c_ctx[None], c, jnp.zeros((MOD_ROWS - 1 - Bl, D), F32)], 0)
    mod = _modulation(cond, w_mod, b_mod)
    ctx_row = lambda b: 0
    lat_row = lambda b: b + 1

    bf = lambda a: a.astype(BF16)
    tr = lambda a: jnp.swapaxes(a, -1, -2)
    w1 = bf(w_mlp1)
    w2 = bf(w_mlp2)
    qmul = HALF ** -0.5 * LOG2E

    tabs64 = _rope_tables(T, DIFF_DH, 0, 2)
    rope64 = (DIFF_DH // 4, tabs64)
    rope64_t = (DIFF_DH // 4, tuple(tr(t) for t in tabs64))

    def qkv_t(x, mod_l, row_of, w_k, w_t, nq, nk, lat, rope, name):
        if lat:
            return _project(x, mod_l, row_of, w_k, w_t, [(0, nk, rope, True, False, None)],
                            [(0, nq, rope, True, False, qmul), (nq, nk, False, True, False, None)],
                            rope=rope64 if rope else None, rope_t=rope64_t if rope else None, name=name)
        return _project(x, mod_l, row_of, w_k, w_t,
                        [(0, nk, False, True, True, None), (nk, nk, False, False, True, None)],
                        [(0, nq, False, True, False, qmul), (nq, nk, False, True, False, None)], name=name)

    def split_w(w, nq, nk):
        return bf(w[:, nq:]), bf(tr(jnp.concatenate([w[:, :nq], w[:, nq + nk:]], 1)))

    w_k, w_t = split_w(l0_w_qkv, D, D)
    wo = bf(l0_w_o)
    lam = l0_lam.astype(F32)
    subln = l0_subln.reshape(2 * DIFF_DH, 1)
    k, k0, v0, qt, vt = qkv_t(x_prompt, mod[0], ctx_row, w_k, w_t, D, D, False, False, "l0_proj_ctx")
    o = _flash(qt, k, vt, mode="diff", extra=(lam, subln), gpb=DIFF_HEADS, name="l0_attn_ctx")
    xp = _post(x_prompt, o, mod[0], ctx_row, ln_g[0], ln_b[0], wo, w1[0], w2[0], shared_mod=True, name="l0_post_ctx")
    k, qt, vt = qkv_t(x_sample, mod[0], lat_row, w_k, w_t, D, D, True, True, "l0_proj_lat")
    cache = (bf(cache_l0_k.reshape(Bl, P, D)), bf(tr(cache_l0_v.reshape(Bl, P, D))))
    o = _flash(qt, k, vt, cache, mode="diff", extra=(lam, subln), name="l0_attn_lat", **DIFF_FLASH)
    xs = _post(x_sample, o, mod[0], lat_row, ln_g[0], ln_b[0], wo, w1[0], w2[0], name="l0_post_lat")

    w_k, w_t = split_w(l1_w_qkv, D, D)
    wo = bf(l1_w_o)
    k, k1, v1, qt, vt = qkv_t(xp, mod[1], ctx_row, w_k, w_t, D, D, False, False, "l1_proj_ctx")
    o = _band(qt, [(k, vt, S, _OWN, None)], n_slab=1, tq=S, gpb=NA_HEADS // 2, name="l1_attn_ctx")
    xp = _post(xp, o, mod[1], ctx_row, ln_g[1], ln_b[1], wo, w1[1], w2[1], shared_mod=True, name="l1_post_ctx")
    k, qt, vt = qkv_t(xs, mod[1], lat_row, w_k, w_t, D, D, True, False, "l1_proj_lat")
    kc, vct = bf(cache_l1_k.reshape(Bl, P, D)), bf(tr(cache_l1_v.reshape(Bl, P, D)))
    tqn = NA_QROWS * GRID_W
    na_bias = _na_bias_blocks(l1_rpb, rows)
    o = _band(qt, [(k, vt, tqn, _PREV, 0), (k, vt, tqn, _OWN, tqn), (k, vt, tqn, _NEXT, 2 * tqn),
                   (kc, vct, P, _FIRST, None)], n_slab=1, tq=tqn, gpb=NA_HEADS // 2, na_bias=na_bias, name="l1_attn_lat")
    xs = _post(xs, o, mod[1], lat_row, ln_g[1], ln_b[1], wo, w1[1], w2[1], name="l1_post_lat")

    H, dq = MLA_HEADS, MLA_NOPE + MLA_ROPE
    wa = bf(jnp.pad(l2_w_a, ((0, 0), (0, -l2_w_a.shape[1] % LANES))))
    wuq = jnp.pad(l2_w_uq.reshape(MLA_Q_RANK, H, dq), ((0, 0), (0, 0), (0, LANES - dq))).reshape(MLA_Q_RANK, H * LANES)
    wuqt = bf(tr(wuq))
    wukv = l2_w_ukv.reshape(MLA_KV_RANK, H, MLA_NOPE + MLA_VDIM)
    wk_n = jnp.pad(wukv[:, :, :MLA_NOPE], ((0, 0), (0, 0), (0, LANES - MLA_NOPE))).reshape(MLA_KV_RANK, H * LANES)
    place = np.zeros((LANES, H, LANES), np.float32)
    place[np.arange(MLA_ROPE), :, MLA_NOPE + np.arange(MLA_ROPE)] = 1.0
    wk = bf(jnp.concatenate([wk_n, jnp.asarray(place.reshape(LANES, H * LANES))], 0))
    wvt = bf(tr(wukv[:, :, MLA_NOPE:].reshape(MLA_KV_RANK, H * MLA_VDIM)))
    wo = bf(l2_w_o)
    qn, kvn = l2_q_norm.reshape(1, -1), l2_kv_norm.reshape(1, -1)
    mscale = dq ** -0.5 * LOG2E
    qt, k, vt, ckv2, kpe2 = _mla_project(xp, mod[2], ctx_row, wa, qn, kvn, wuqt, wk, wvt, mscale, name="l2_proj_ctx")
    o = _flash(qt, k, vt, mode="mla", gpb=MLA_HEADS // 2, name="l2_attn_ctx")
    xp = _post(xp, o, mod[2], ctx_row, ln_g[2], ln_b[2], wo, w1[2], w2[2], shared_mod=True, name="l2_post_ctx")
    tabs = tuple(tr(t) for t in _rope_tables(T, MLA_ROPE, MLA_NOPE, 1)) + _rope_tables(T, MLA_ROPE, 0, 1)
    qt, k, vt = _mla_project(xs, mod[2], lat_row, wa, qn, kvn, wuqt, wk, wvt, mscale, rope=tabs, name="l2_proj_lat")
    kpe_slab = jnp.pad(cache_l2_kpe, ((0, 0), (0, 0), (0, LANES - MLA_ROPE)))
    cache = _mla_cache_expand(cache_l2_ckv, kpe_slab, wk, wvt)
    o = _flash(qt, k, vt, cache, mode="mla", name="l2_attn_lat", **MLA_FLASH)
    xs = _post(xs, o, mod[2], lat_row, ln_g[2], ln_b[2], wo, w1[2], w2[2], name="l2_post_lat")

    nq3, nk3 = SWA_HEADS * HALF, SWA_KV_HEADS * HALF
    w = jnp.concatenate([_perm_heads(l3_w_qkv[:, :nq3], 1), l3_w_qkv[:, nq3:]], 1)
    w_k, w_t = split_w(w, nq3, nk3)
    wo = bf(_perm_heads(l3_w_o, 0))
    sink = jnp.take(l3_sink.astype(F32), jnp.asarray(_SWA_PERM)) * LOG2E

    def sink_rows(tq):
        return jnp.repeat(sink.reshape(2, 1, 8), tq, axis=-1)

    k, k3, v3, qt, vt = qkv_t(xp, mod[3], ctx_row, w_k, w_t, nq3, nk3, False, False, "l3_proj_ctx")
    o = _band(qt, [(k, vt, S, _OWN, None)], n_slab=4, tq=S, gpb=2, sink=sink_rows(S), name="l3_attn_ctx")
    xp = _post(xp, o, mod[3], ctx_row, ln_g[3], ln_b[3], wo, w1[3], w2[3], shared_mod=True, name="l3_post_ctx")
    k, qt, vt = qkv_t(xs, mod[3], lat_row, w_k, w_t, nq3, nk3, True, True, "l3_proj_lat")
    kc, vct = bf(cache_l3_k.reshape(Bl, P, nk3)), bf(tr(cache_l3_v.reshape(Bl, P, nk3)))
    mask = _swa_mask_tables(T // SWA_QB, 8)
    o = _band(qt, [(k, vt, SWA_QB, _PREV, 0), (k, vt, SWA_QB, _OWN, SWA_QB), (k, vt, SWA_QB, _NEXT, 2 * SWA_QB),
                   (kc, vct, P, _FIRST, None)], n_slab=4, tq=SWA_QB, gpb=2, bias=mask, sink=sink_rows(SWA_QB),
              name="l3_attn_lat")
    xs = _post(xs, o, mod[3], lat_row, ln_g[3], ln_b[3], wo, w1[3], w2[3], name="l3_post_lat")

    return (xp, xs,
            k0.reshape(Bc, S, DIFF_HEADS, 2 * DIFF_DH), v0.reshape(Bc, S, DIFF_HEADS, 2 * DIFF_DH),
            k1.reshape(Bc, S, NA_HEADS, HALF), v1.reshape(Bc, S, NA_HEADS, HALF),
            ckv2, kpe2,
            k3.reshape(Bc, S, SWA_KV_HEADS, HALF), v3.reshape(Bc, S, SWA_KV_HEADS, HALF))
```
